```python
import math, functools
import jax, jax.numpy as jnp
from jax import lax
import numpy as np

D_MODEL = 4096
BATCH = 4
SEQ = 2048
DEPTH = 1
DEC_BATCH = 128
DEC_SEQ = 4
PAST_LEN = 2048
PAGE_SIZE = 128

HEAD_DIM = 128
FOX_HEADS = D_MODEL // 256
FOX_KV_HEADS = FOX_HEADS // 2
FOX_GROUPS = FOX_HEADS // FOX_KV_HEADS
FOX_WIDTH = FOX_HEADS * HEAD_DIM
FOX_KV_WIDTH = FOX_KV_HEADS * HEAD_DIM
GDN_HEADS = D_MODEL // 256
GDN_DK = 128
GDN_DV = 128
GDN_KEY_WIDTH = GDN_HEADS * GDN_DK
GDN_VAL_WIDTH = GDN_HEADS * GDN_DV
GDN_CONV_WIDTH = 2 * GDN_KEY_WIDTH + GDN_VAL_WIDTH
GDN_CONV = 4
GDN_CHUNK = 64
D_FF = 256 * ((8 * D_MODEL // 3 + 255) // 256)
FFN_CONV = 3
Q_BLOCK = 128
RMS_EPS = 1e-6
L2_EPS = 1e-6
NEG_INF = -1e30

_IN_SIZES = (FOX_WIDTH, FOX_KV_WIDTH, FOX_KV_WIDTH, FOX_HEADS,
             GDN_CONV_WIDTH, GDN_VAL_WIDTH, GDN_HEADS, GDN_HEADS,
             D_MODEL, D_MODEL)
IN_WIDTH = sum(_IN_SIZES)
IN_SPLITS = tuple(sum(_IN_SIZES[:i + 1]) for i in range(len(_IN_SIZES) - 1))

kernel_name = 'fox_gdn_convffn_hybrid_step'


def rmsnorm(x, g):
    xf = x.astype(jnp.float32)
    y = xf * lax.rsqrt(jnp.mean(xf * xf, axis=-1, keepdims=True) + RMS_EPS)
    return (y * g.astype(jnp.float32)).astype(x.dtype)


def l2norm(x):
    xf = x.astype(jnp.float32)
    return xf * lax.rsqrt(jnp.sum(xf * xf, axis=-1, keepdims=True) + L2_EPS)


def causal_dwconv(x, buf, w):
    width = w.shape[0]
    T = x.shape[1]
    xp = jnp.concatenate([buf.astype(x.dtype), x], axis=1)
    y = xp[:, 0:T] * w[0]
    for i in range(1, width):
        y = y + xp[:, i:i + T] * w[i]
    return y, xp[:, xp.shape[1] - (width - 1):]


def ada_modulation(c, w_ada, b_ada):
    m = jax.nn.silu(c) @ w_ada + b_ada
    return jnp.split(m[:, None, :], 6, axis=-1)


def heads_first(F):
    B, T = F.shape[:2]
    return jnp.transpose(F, (0, 2, 1)).reshape(B, FOX_KV_HEADS, FOX_GROUPS, T)


def fox_prompt(q, k, v, logf):
    B, T = q.shape[:2]
    F = heads_first(jnp.cumsum(logf, axis=1))
    nb = T // Q_BLOCK
    qb = jnp.transpose(q.reshape(B, nb, Q_BLOCK, FOX_KV_HEADS, FOX_GROUPS, HEAD_DIM), (1, 0, 2, 3, 4, 5))
    Fqb = jnp.transpose(F.reshape(B, FOX_KV_HEADS, FOX_GROUPS, nb, Q_BLOCK), (3, 0, 1, 2, 4))
    pos_k = jnp.arange(T)
    scale = HEAD_DIM ** -0.5

    def block(args):
        i, q_i, Fq_i = args
        pos_q = i * Q_BLOCK + jnp.arange(Q_BLOCK)
        s = jnp.einsum('bqkgd,bskd->bkgqs', q_i, k, preferred_element_type=jnp.float32) * scale
        s = s + Fq_i[..., :, None] - F[..., None, :]
        s = jnp.where(pos_k[None, :] <= pos_q[:, None], s, NEG_INF)
        p = jax.nn.softmax(s, axis=-1)
        return jnp.einsum('bkgqs,bskd->bqkgd', p.astype(v.dtype), v)

    o = lax.map(block, (jnp.arange(nb), qb, Fqb))
    return jnp.transpose(o, (1, 0, 2, 3, 4, 5)).reshape(B, T, FOX_WIDTH)


def fox_sample(q, k, v, logf, k_past, v_past, logf_past):
    B, T = q.shape[:2]
    P = k_past.shape[1]
    F_past = jnp.cumsum(logf_past.astype(jnp.float32), axis=1)
    F_new = F_past[:, P - 1:P] + jnp.cumsum(logf, axis=1)
    Fp = heads_first(F_past)
    Fn = heads_first(F_new)
    qg = q.reshape(B, T, FOX_KV_HEADS, FOX_GROUPS, HEAD_DIM)
    scale = HEAD_DIM ** -0.5
    s_past = jnp.einsum('bqkgd,bskd->bkgqs', qg, k_past.astype(q.dtype), preferred_element_type=jnp.float32) * scale
    s_past = s_past + Fn[..., :, None] - Fp[..., None, :]
    s_new = jnp.einsum('bqkgd,bskd->bkgqs', qg, k, preferred_element_type=jnp.float32) * scale
    s_new = s_new + Fn[..., :, None] - Fn[..., None, :]
    idx = jnp.arange(T)
    s_new = jnp.where(idx[None, :] <= idx[:, None], s_new, NEG_INF)
    p = jax.nn.softmax(jnp.concatenate([s_past, s_new], axis=-1), axis=-1)
    o = (jnp.einsum('bkgqs,bskd->bqkgd', p[..., :P].astype(v.dtype), v_past.astype(v.dtype))
         + jnp.einsum('bkgqs,bskd->bqkgd', p[..., P:].astype(v.dtype), v))
    return o.reshape(B, T, FOX_WIDTH)


def gdn_chunked(q, k, v, g, beta, S0):
    B, T = q.shape[:2]
    C = min(GDN_CHUNK, T)
    n = -(-T // C)
    pad = n * C - T

    def chunks(a):
        a = jnp.pad(a.astype(jnp.float32), [(0, 0), (0, pad)] + [(0, 0)] * (a.ndim - 2))
        return jnp.moveaxis(a.reshape((B, n, C) + a.shape[2:]), 3, 1)

    q, k, v, g, beta = (chunks(a) for a in (q, k, v, g, beta))
    H = q.shape[1]
    gc = jnp.cumsum(g, axis=-1)
    idx = jnp.arange(C)
    incl = idx[:, None] >= idx[None, :]
    strict = idx[:, None] > idx[None, :]
    decay = jnp.exp(jnp.where(incl, gc[..., :, None] - gc[..., None, :], NEG_INF))
    kb = k * beta[..., None]
    M = jnp.where(strict, jnp.einsum('bhnck,bhnsk->bhncs', kb, k) * decay, 0.0)
    eye = jnp.eye(C, dtype=jnp.float32)
    Tinv = lax.linalg.triangular_solve(eye + M, jnp.broadcast_to(eye, M.shape), left_side=True, lower=True)
    u = jnp.einsum('bhncs,bhnsv->bhncv', Tinv, v * beta[..., None])
    w = jnp.einsum('bhncs,bhnsk->bhnck', Tinv, kb * jnp.exp(gc)[..., None])
    A = jnp.einsum('bhnck,bhnsk->bhncs', q, k) * decay
    qd = q * jnp.exp(gc)[..., None]
    kd = k * jnp.exp(gc[..., C - 1:] - gc)[..., None]
    glast = jnp.exp(gc[..., C - 1])

    def step(S, xs):
        u_c, w_c, qd_c, kd_c, A_c, gl_c = xs
        v_new = u_c - jnp.einsum('bhck,bhkv->bhcv', w_c, S)
        o_c = jnp.einsum('bhck,bhkv->bhcv', qd_c, S) + jnp.einsum('bhcs,bhsv->bhcv', A_c, v_new)
        S = S * gl_c[..., None, None] + jnp.einsum('bhck,bhcv->bhkv', kd_c, v_new)
        return S, o_c

    xs = tuple(jnp.moveaxis(a, 2, 0) for a in (u, w, qd, kd, A, glast))
    S, o = lax.scan(step, S0.astype(jnp.float32), xs)
    o = jnp.transpose(o, (1, 0, 3, 2, 4)).reshape(B, n * C, H, GDN_DV)[:, :T]
    return o, S


def gdn_branch(qkv_raw, z, a, b, conv_buf, S0, conv_w, A_log, dt_bias, norm_w):
    B, T = qkv_raw.shape[:2]
    qkv, new_buf = causal_dwconv(qkv_raw, conv_buf, conv_w)
    qkv = jax.nn.silu(qkv)
    qr, kr, vr = jnp.split(qkv, [GDN_KEY_WIDTH, 2 * GDN_KEY_WIDTH], axis=-1)
    q = l2norm(qr.reshape(B, T, GDN_HEADS, GDN_DK)) * (GDN_DK ** -0.5)
    k = l2norm(kr.reshape(B, T, GDN_HEADS, GDN_DK))
    v = vr.reshape(B, T, GDN_HEADS, GDN_DV)
    beta = jax.nn.sigmoid(b.astype(jnp.float32))
    g = -jnp.exp(A_log.astype(jnp.float32)) * jax.nn.softplus(a.astype(jnp.float32) + dt_bias.astype(jnp.float32))
    o, S = gdn_chunked(q, k, v, g, beta, S0)
    o = rmsnorm(o, norm_w) * jax.nn.silu(z.reshape(B, T, GDN_HEADS, GDN_DV).astype(jnp.float32))
    return o.reshape(B, T, GDN_VAL_WIDTH).astype(qkv_raw.dtype), S, new_buf


def conv_ffn(h, conv_buf, w_gate, w_up, conv_w, w_down):
    gate, new_buf = causal_dwconv(h @ w_gate, conv_buf, conv_w)
    return (jax.nn.silu(gate) * (h @ w_up)) @ w_down, new_buf


def decoder_layer(x, c, fox_fn, gdn_S0, gdn_buf0, ffn_buf0,
                  w_ada, b_ada, norm_mix, norm_ffn, w_in, fox_b_f, fox_q_norm, fox_k_norm,
                  gdn_conv_w, gdn_A_log, gdn_dt_bias, gdn_norm, w_branch_fox, w_branch_gdn, w_out,
                  ffn_w_gate, ffn_w_up, ffn_conv_w, ffn_w_down):
    B, T, _ = x.shape
    sh_m, sc_m, gt_m, sh_f, sc_f, gt_f = ada_modulation(c, w_ada, b_ada)
    h = rmsnorm(x, norm_mix) * (1 + sc_m) + sh_m
    proj = h @ w_in
    fq, fk, fv, ff, g_qkv, g_z, g_a, g_b, gate_fox, gate_gdn = jnp.split(proj, IN_SPLITS, axis=-1)
    q = rmsnorm(fq.reshape(B, T, FOX_HEADS, HEAD_DIM), fox_q_norm)
    k = rmsnorm(fk.reshape(B, T, FOX_KV_HEADS, HEAD_DIM), fox_k_norm)
    v = fv.reshape(B, T, FOX_KV_HEADS, HEAD_DIM)
    logf = jax.nn.log_sigmoid(ff.astype(jnp.float32) + fox_b_f.astype(jnp.float32))
    o_fox = fox_fn(q, k, v, logf)
    o_gdn, S, gbuf = gdn_branch(g_qkv, g_z, g_a, g_b, gdn_buf0, gdn_S0,
                                gdn_conv_w, gdn_A_log, gdn_dt_bias, gdn_norm)
    merged = jax.nn.sigmoid(gate_fox) * (o_fox @ w_branch_fox) + jax.nn.sigmoid(gate_gdn) * (o_gdn @ w_branch_gdn)
    x = x + gt_m * (merged @ w_out)
    h = rmsnorm(x, norm_ffn) * (1 + sc_f) + sh_f
    ffn, fbuf = conv_ffn(h, ffn_buf0, ffn_w_gate, ffn_w_up, ffn_conv_w, ffn_w_down)
    x = x + gt_f * ffn
    return x, (k, v, logf, S, gbuf, fbuf)


def setup_inputs(seed: int = 0) -> dict:
    key = jax.random.key(seed)
    keys = iter(jax.random.split(key, 40))

    def normal(shape, scale=1.0):
        return scale * jax.random.normal(next(keys), shape, jnp.float32)

    def gain(shape):
        return 1.0 + normal(shape, 0.02)

    n_pages = PAST_LEN // PAGE_SIZE
    n_used = DEC_BATCH * n_pages
    n_pool = n_used + max(1, n_used // 4)
    x_prompt = normal((BATCH, SEQ, D_MODEL))
    x_sample = normal((DEC_BATCH, DEC_SEQ, D_MODEL))
    cache_k = normal((DEPTH, n_pool, PAGE_SIZE, FOX_KV_HEADS, HEAD_DIM))
    cache_v = normal((DEPTH, n_pool, PAGE_SIZE, FOX_KV_HEADS, HEAD_DIM))
    cache_logf = jax.nn.log_sigmoid(normal((DEPTH, n_pool, PAGE_SIZE, FOX_HEADS)) + 3.0)
    state_gdn = normal((DEPTH, DEC_BATCH, GDN_HEADS, GDN_DK, GDN_DV), 0.1)
    state_gdn_conv = normal((DEPTH, DEC_BATCH, GDN_CONV - 1, GDN_CONV_WIDTH))
    state_ffn_conv = normal((DEPTH, DEC_BATCH, FFN_CONV - 1, D_FF))
    page_table = jax.random.permutation(next(keys), n_pool)[:n_used].reshape(DEC_BATCH, n_pages).astype(jnp.int32)
    c_prompt = normal((BATCH, D_MODEL))
    c_sample = normal((DEC_BATCH, D_MODEL))
    dt = jnp.exp(jax.random.uniform(next(keys), (DEPTH, GDN_HEADS), jnp.float32, math.log(1e-3), math.log(1e-1)))
    gdn_dt_bias = dt + jnp.log(-jnp.expm1(-dt))
    gdn_A_log = jnp.log(jax.random.uniform(next(keys), (DEPTH, GDN_HEADS), jnp.float32, 1.0, 16.0))
    return {
        'x_prompt': x_prompt,
        'x_sample': x_sample,
        'cache_k': cache_k,
        'cache_v': cache_v,
        'cache_logf': cache_logf,
        'state_gdn': state_gdn,
        'state_gdn_conv': state_gdn_conv,
        'state_ffn_conv': state_ffn_conv,
        'page_table': page_table,
        'c_prompt': c_prompt,
        'c_sample': c_sample,
        'w_ada': normal((DEPTH, D_MODEL, 6 * D_MODEL), D_MODEL ** -0.5),
        'b_ada': normal((DEPTH, 6 * D_MODEL), 0.02),
        'norm_mix': gain((DEPTH, D_MODEL)),
        'norm_ffn': gain((DEPTH, D_MODEL)),
        'w_in': normal((DEPTH, D_MODEL, IN_WIDTH), D_MODEL ** -0.5),
        'fox_b_f': 3.0 + normal((DEPTH, FOX_HEADS), 0.5),
        'fox_q_norm': gain((DEPTH, HEAD_DIM)),
        'fox_k_norm': gain((DEPTH, HEAD_DIM)),
        'gdn_conv_w': normal((DEPTH, GDN_CONV, GDN_CONV_WIDTH), GDN_CONV ** -0.5),
        'gdn_A_log': gdn_A_log,
        'gdn_dt_bias': gdn_dt_bias,
        'gdn_norm': gain((DEPTH, GDN_DV)),
        'w_branch_fox': normal((DEPTH, FOX_WIDTH, D_MODEL), FOX_WIDTH ** -0.5),
        'w_branch_gdn': normal((DEPTH, GDN_VAL_WIDTH, D_MODEL), GDN_VAL_WIDTH ** -0.5),
        'w_out': normal((DEPTH, D_MODEL, D_MODEL), D_MODEL ** -0.5),
        'ffn_w_gate': normal((DEPTH, D_MODEL, D_FF), D_MODEL ** -0.5),
        'ffn_w_up': normal((DEPTH, D_MODEL, D_FF), D_MODEL ** -0.5),
        'ffn_conv_w': normal((DEPTH, FFN_CONV, D_FF), FFN_CONV ** -0.5),
        'ffn_w_down': normal((DEPTH, D_FF, D_MODEL), D_FF ** -0.5),
    }


def reference(x_prompt, x_sample, cache_k, cache_v, cache_logf, state_gdn, state_gdn_conv, state_ffn_conv,
              page_table, c_prompt, c_sample, w_ada, b_ada, norm_mix, norm_ffn, w_in, fox_b_f, fox_q_norm,
              fox_k_norm, gdn_conv_w, gdn_A_log, gdn_dt_bias, gdn_norm, w_branch_fox, w_branch_gdn, w_out,
              ffn_w_gate, ffn_w_up, ffn_conv_w, ffn_w_down):
    n_pages = page_table.shape[1]
    n_prompt = x_prompt.shape[0]
    n_dec = x_sample.shape[0]
    past = n_pages * PAGE_SIZE
    y_prompt, y_sample = x_prompt, x_sample
    new_prompt = [[] for _ in range(6)]
    new_sample = [[] for _ in range(6)]
    for l in range(DEPTH):
        layer_w = (w_ada[l], b_ada[l], norm_mix[l], norm_ffn[l], w_in[l], fox_b_f[l], fox_q_norm[l], fox_k_norm[l],
                   gdn_conv_w[l], gdn_A_log[l], gdn_dt_bias[l], gdn_norm[l], w_branch_fox[l], w_branch_gdn[l], w_out[l],
                   ffn_w_gate[l], ffn_w_up[l], ffn_conv_w[l], ffn_w_down[l])
        zeros_S = jnp.zeros((n_prompt, GDN_HEADS, GDN_DK, GDN_DV), jnp.float32)
        zeros_gconv = jnp.zeros((n_prompt, GDN_CONV - 1, GDN_CONV_WIDTH), x_prompt.dtype)
        zeros_fconv = jnp.zeros((n_prompt, FFN_CONV - 1, D_FF), x_prompt.dtype)
        y_prompt, st_p = decoder_layer(y_prompt, c_prompt, fox_prompt, zeros_S, zeros_gconv, zeros_fconv, *layer_w)
        k_past = cache_k[l, page_table].reshape(n_dec, past, FOX_KV_HEADS, HEAD_DIM)
        v_past = cache_v[l, page_table].reshape(n_dec, past, FOX_KV_HEADS, HEAD_DIM)
        logf_past = cache_logf[l, page_table].reshape(n_dec, past, FOX_HEADS)
        fox_fn = functools.partial(fox_sample, k_past=k_past, v_past=v_past, logf_past=logf_past)
        y_sample, st_s = decoder_layer(y_sample, c_sample, fox_fn, state_gdn[l], state_gdn_conv[l],
                                       state_ffn_conv[l], *layer_w)
        for i in range(6):
            new_prompt[i].append(st_p[i])
            new_sample[i].append(st_s[i])
    k_p, v_p, lf_p, S_p, gc_p, fc_p = (jnp.stack(a) for a in new_prompt)
    k_s, v_s, lf_s, S_s, gc_s, fc_s = (jnp.stack(a) for a in new_sample)
    return (y_prompt, y_sample, k_p, v_p, lf_p, S_p, gc_p, fc_p, k_s, v_s, lf_s, S_s, gc_s, fc_s)
```

```python
import functools

import jax
import jax.numpy as jnp
from jax import lax
from jax.experimental import pallas as pl
from jax.experimental.pallas import tpu as pltpu

F32 = jnp.float32
BF16 = jnp.bfloat16
RMS_EPS = 1e-6
L2_EPS = 1e-6
NEG_INF = -1e30
LANE = 128
SUBLANE = 8
BF16_ROWS = 16
VMEM_LIMIT_BYTES = 56 * 2 ** 20

NN = (((1,), (0,)), ((), ()))
NT = (((1,), (1,)), ((), ()))
TN = (((0,), (0,)), ((), ()))


def _pick(n, pref, align):
    t = min(pref, n) // align * align
    while t >= align:
        if n % t == 0:
            return t
        t -= align
    return n


def _params(*sem):
    return pltpu.CompilerParams(dimension_semantics=sem, vmem_limit_bytes=VMEM_LIMIT_BYTES)


def _dot(a, b, dims=NN):
    return lax.dot_general(a.astype(BF16), b.astype(BF16), dims, preferred_element_type=F32)


def _split2(a):
    hi = a.astype(BF16)
    lo = (a - hi.astype(F32)).astype(BF16)
    return hi, lo


def _dot3(a, b, dims=NN):
    ah, al = _split2(a)
    bh, bl = _split2(b)
    d = functools.partial(lax.dot_general, dimension_numbers=dims, preferred_element_type=F32)
    return d(ah, bh) + (d(ah, bl) + d(al, bh))


def _dot_sel(sel, x, sel_first=True):
    h = x.astype(BF16)
    r = x - h.astype(F32)
    m = r.astype(BF16)
    l = (r - m.astype(F32)).astype(BF16)
    if sel_first:
        d = lambda p: lax.dot_general(sel, p, NN, preferred_element_type=F32)
    else:
        d = lambda p: lax.dot_general(p, sel, NN, preferred_element_type=F32)
    return d(h) + (d(m) + d(l))


def _silu(x):
    return x * jax.nn.sigmoid(x)


def _softplus(x):
    return jnp.maximum(x, 0.0) + jnp.log1p(jnp.exp(-jnp.abs(x)))


def _rows(a, m):
    if m.shape[0] == 1 or m.shape[0] == a.shape[0]:
        return m
    reps = a.shape[0] // m.shape[0]
    return jnp.concatenate([m] * reps, axis=0)


def _ada_kernel(c_ref, w_ref, b_ref, o_ref):
    a = _silu(c_ref[...])
    o_ref[...] = _dot(a, w_ref[...]) + b_ref[...]


def _ada(c_all, w_ada, b_ada):
    mc, d = c_all.shape
    n = w_ada.shape[1]
    tn = _pick(n, 512, LANE)
    return pl.pallas_call(
        _ada_kernel,
        grid=(n // tn,),
        in_specs=[pl.BlockSpec((mc, d), lambda j: (0, 0)),
                  pl.BlockSpec((d, tn), lambda j: (0, j)),
                  pl.BlockSpec((1, tn), lambda j: (0, j))],
        out_specs=pl.BlockSpec((mc, tn), lambda j: (0, j)),
        out_shape=jax.ShapeDtypeStruct((mc, n), F32),
        compiler_params=_params("parallel"),
        name="ada",
    )(c_all, w_ada, b_ada.reshape(1, n))


def _normmod_kernel(x_ref, g_ref, sc_ref, sh_ref, o_ref):
    x = x_ref[0]
    y = x * lax.rsqrt(jnp.mean(x * x, axis=-1, keepdims=True) + RMS_EPS) * g_ref[...]
    o_ref[0] = (y * (1.0 + _rows(y, sc_ref[0])) + _rows(y, sh_ref[0])).astype(o_ref.dtype)


def _normmod(x3, gain, sc3, sh3, tiles_per_mod):
    nt, tr, d = x3.shape
    mr = sc3.shape[1]
    mod_spec = pl.BlockSpec((1, mr, d), lambda i: (i // tiles_per_mod, 0, 0))
    return pl.pallas_call(
        _normmod_kernel,
        grid=(nt,),
        in_specs=[pl.BlockSpec((1, tr, d), lambda i: (i, 0, 0)),
                  pl.BlockSpec((1, d), lambda i: (0, 0)),
                  mod_spec, mod_spec],
        out_specs=pl.BlockSpec((1, tr, d), lambda i: (i, 0, 0)),
        out_shape=jax.ShapeDtypeStruct((nt, tr, d), BF16),
        compiler_params=_params("parallel"),
        name="normmod",
    )(x3, gain.reshape(1, d), sc3, sh3)


def _mm_kernel(x_ref, w_ref, o_ref):
    o_ref[...] = jnp.dot(x_ref[...], w_ref[...], preferred_element_type=F32).astype(o_ref.dtype)


def _matmul(x, w, tm, tn, out_dtype=F32):
    m, k = x.shape
    n = w.shape[1]
    return pl.pallas_call(
        _mm_kernel,
        grid=(m // tm, n // tn),
        in_specs=[pl.BlockSpec((tm, k), lambda i, j: (i, 0)),
                  pl.BlockSpec((k, tn), lambda i, j: (0, j))],
        out_specs=pl.BlockSpec((tm, tn), lambda i, j: (i, j)),
        out_shape=jax.ShapeDtypeStruct((m, n), out_dtype),
        compiler_params=_params("parallel", "arbitrary"),
        name="inproj",
    )(x, w)


def _small_kernel(x_ref, w_ref, p_ref, o_ref, *, nh):
    acc = jnp.dot(x_ref[...], w_ref[...], preferred_element_type=F32)
    xb = acc + p_ref[0:1, :]
    lane = lax.broadcasted_iota(jnp.int32, acc.shape, 1)
    sp = _softplus(xb)
    logf = jnp.minimum(xb, 0.0) - jnp.log1p(jnp.exp(-jnp.abs(xb)))
    g = -jnp.exp(p_ref[1:2, :]) * sp
    beta = jax.nn.sigmoid(acc)
    o_ref[...] = jnp.where(lane < nh, logf, jnp.where(lane < 2 * nh, g, jnp.where(lane < 3 * nh, beta, 0.0)))


def _small_heads(h, w_small, p_small, nh, tm):
    m, k = h.shape
    return pl.pallas_call(
        functools.partial(_small_kernel, nh=nh),
        grid=(m // tm,),
        in_specs=[pl.BlockSpec((tm, k), lambda i: (i, 0)),
                  pl.BlockSpec((k, LANE), lambda i: (0, 0)),
                  pl.BlockSpec((SUBLANE, LANE), lambda i: (0, 0))],
        out_specs=pl.BlockSpec((tm, LANE), lambda i: (i, 0)),
        out_shape=jax.ShapeDtypeStruct((m, LANE), F32),
        compiler_params=_params("parallel"),
        name="small_heads",
    )(h, w_small, p_small)


def _qknorm_kernel(x_ref, qg_ref, kg_ref, qn_ref, kn_ref, vb_ref, kf_ref, vf_ref, *, nq, nkv):
    def norm(x, g):
        return x * lax.rsqrt(jnp.mean(x * x, axis=-1, keepdims=True) + RMS_EPS) * g

    for h in range(nq):
        sl = slice(h * LANE, (h + 1) * LANE)
        qn_ref[:, sl] = norm(x_ref[:, sl], qg_ref[...]).astype(BF16)
    for h in range(nkv):
        sl = slice(h * LANE, (h + 1) * LANE)
        kx = norm(x_ref[:, (nq + h) * LANE:(nq + h + 1) * LANE], kg_ref[...])
        kf_ref[:, sl] = kx
        kn_ref[:, sl] = kx.astype(BF16)
        vx = x_ref[:, (nq + nkv + h) * LANE:(nq + nkv + h + 1) * LANE]
        vf_ref[:, sl] = vx
        vb_ref[:, sl] = vx.astype(BF16)


def _qknorm(proj, q_gain, k_gain, nq, nkv, tm):
    m = proj.shape[0]
    wq, wk = nq * LANE, nkv * LANE
    row = lambda w: pl.BlockSpec((tm, w), lambda i: (i, 0))
    gain = pl.BlockSpec((1, LANE), lambda i: (0, 0))
    return pl.pallas_call(
        functools.partial(_qknorm_kernel, nq=nq, nkv=nkv),
        grid=(m // tm,),
        in_specs=[row(wq + 2 * wk), gain, gain],
        out_specs=[row(wq), row(wk), row(wk), row(wk), row(wk)],
        out_shape=[jax.ShapeDtypeStruct((m, wq), BF16), jax.ShapeDtypeStruct((m, wk), BF16),
                   jax.ShapeDtypeStruct((m, wk), BF16), jax.ShapeDtypeStruct((m, wk), F32),
                   jax.ShapeDtypeStruct((m, wk), F32)],
        compiler_params=_params("parallel"),
        name="qknorm",
    )(proj, q_gain.reshape(1, LANE), k_gain.reshape(1, LANE))


def _cumf_kernel(x_ref, o_ref, *, blk):
    t = x_ref.shape[0]
    r = lax.broadcasted_iota(jnp.int32, (blk, blk), 0)
    c = lax.broadcasted_iota(jnp.int32, (blk, blk), 1)
    tri = (r >= c).astype(BF16)
    carry = jnp.zeros((1, LANE), F32)
    for i in range(t // blk):
        cum = _dot_sel(tri, x_ref[i * blk:(i + 1) * blk, :]) + carry
        carry = cum[blk - 1:blk, :]
        o_ref[0, :, i * blk:(i + 1) * blk] = cum.T


def _cumf(small, nb, t):
    blk = _pick(t, 256, LANE)
    return pl.pallas_call(
        functools.partial(_cumf_kernel, blk=blk),
        grid=(nb,),
        in_specs=[pl.BlockSpec((t, LANE), lambda b: (b, 0))],
        out_specs=pl.BlockSpec((1, LANE, t), lambda b: (b, 0, 0)),
        out_shape=jax.ShapeDtypeStruct((nb, LANE, t), F32),
        compiler_params=_params("parallel"),
        name="cumf",
    )(small)


def _fox_prompt_kernel(q_ref, k_ref, v_ref, f_ref, o_ref, m_sc, l_sc, acc_sc, *, g, tq, tk, scale):
    qi, ki = pl.program_id(2), pl.program_id(3)
    nk = pl.num_programs(3)

    @pl.when(ki == 0)
    def _():
        m_sc[...] = jnp.full(m_sc.shape, NEG_INF, F32)
        l_sc[...] = jnp.zeros(l_sc.shape, F32)
        acc_sc[...] = jnp.zeros(acc_sc.shape, F32)

    @pl.when(ki * tk <= qi * tq + tq - 1)
    def _():
        qs = jnp.concatenate([q_ref[:, i * LANE:(i + 1) * LANE] for i in range(g)], axis=0)
        s = lax.dot_general(qs, k_ref[...], NT, preferred_element_type=F32) * scale
        bias = jnp.concatenate([jnp.broadcast_to(f_ref[0, 0, i:i + 1, :], (tq, tk)) for i in range(g)], axis=0)
        s = s - bias
        row = lax.broadcasted_iota(jnp.int32, (tq, tk), 0)
        col = lax.broadcasted_iota(jnp.int32, (tq, tk), 1)
        keep = (ki * tk + col) <= (qi * tq + row)
        s = jnp.where(jnp.concatenate([keep] * g, axis=0), s, NEG_INF)
        m_prev = m_sc[...]
        m_new = jnp.maximum(m_prev, jnp.max(s, axis=-1, keepdims=True))
        alpha = jnp.exp(m_prev - m_new)
        p = jnp.exp(s - m_new)
        l_sc[...] = alpha * l_sc[...] + jnp.sum(p, axis=-1, keepdims=True)
        acc_sc[...] = alpha * acc_sc[...] + jnp.dot(p.astype(BF16), v_ref[...], preferred_element_type=F32)
        m_sc[...] = m_new

    @pl.when(ki == nk - 1)
    def _():
        o = acc_sc[...] / l_sc[...]
        for i in range(g):
            o_ref[:, i * LANE:(i + 1) * LANE] = o[i * tq:(i + 1) * tq].astype(o_ref.dtype)


def _fox_prompt(qn, kn, vb, fpad, nb, t, nkv, g):
    tq = _pick(t, 256, LANE)
    tk = tq
    nq, nk = t // tq, t // tk
    last = lambda qi: (qi * tq + tq - 1) // tk
    kv_spec = pl.BlockSpec((tk, LANE), lambda b, h, qi, ki: (b * nk + jnp.minimum(ki, last(qi)), h))
    return pl.pallas_call(
        functools.partial(_fox_prompt_kernel, g=g, tq=tq, tk=tk, scale=LANE ** -0.5),
        grid=(nb, nkv, nq, nk),
        in_specs=[pl.BlockSpec((tq, g * LANE), lambda b, h, qi, ki: (b * nq + qi, h)),
                  kv_spec, kv_spec,
                  pl.BlockSpec((1, 1, SUBLANE, tk), lambda b, h, qi, ki: (b, h, 0, jnp.minimum(ki, last(qi))))],
        out_specs=pl.BlockSpec((tq, g * LANE), lambda b, h, qi, ki: (b * nq + qi, h)),
        out_shape=jax.ShapeDtypeStruct((nb * t, nkv * g * LANE), BF16),
        scratch_shapes=[pltpu.VMEM((g * tq, 1), F32), pltpu.VMEM((g * tq, 1), F32),
                        pltpu.VMEM((g * tq, LANE), F32)],
        compiler_params=_params("parallel", "parallel", "parallel", "arbitrary"),
        name="fox_prompt",
    )(qn, kn, vb, fpad)


def _fox_sample_kernel(pt_ref, q_ref, kn_ref, vn_ref, lfn_ref, *refs, npg, nkv, g, ts, scale):
    k_refs, v_refs, lf_refs = refs[:npg], refs[npg:2 * npg], refs[2 * npg:3 * npg]
    o_ref, m_sc, l_sc, acc_sc, f_sc = refs[3 * npg:]
    p = pl.program_id(1)
    page = k_refs[0].shape[1]
    gt = g * ts

    @pl.when(p == 0)
    def _():
        m_sc[...] = jnp.full(m_sc.shape, NEG_INF, F32)
        l_sc[...] = jnp.zeros(l_sc.shape, F32)
        acc_sc[...] = jnp.zeros(acc_sc.shape, F32)
        f_sc[...] = jnp.zeros(f_sc.shape, F32)

    r = lax.broadcasted_iota(jnp.int32, (page, page), 0)
    c = lax.broadcasted_iota(jnp.int32, (page, page), 1)
    upper = (r <= c).astype(BF16)

    def q_head(h):
        return jnp.concatenate([q_ref[0, :, (h * g + i) * LANE:(h * g + i + 1) * LANE] for i in range(g)], axis=0)

    def bias_rows(fcum, h, width):
        return jnp.concatenate(
            [jnp.broadcast_to(fcum[h * g + i:h * g + i + 1, :width], (ts, width)) for i in range(g)], axis=0)

    def update(h, s, pv):
        rows = slice(h * gt, (h + 1) * gt)
        m_prev = m_sc[rows]
        m_new = jnp.maximum(m_prev, jnp.max(s, axis=-1, keepdims=True))
        alpha = jnp.exp(m_prev - m_new)
        pr = jnp.exp(s - m_new)
        l_sc[rows] = alpha * l_sc[rows] + jnp.sum(pr, axis=-1, keepdims=True)
        acc_sc[rows] = alpha * acc_sc[rows] + pv(pr)
        m_sc[rows] = m_new

    carry = f_sc[...]
    fcums = []
    for i in range(npg):
        fc = _dot_sel(upper, lf_refs[i][0], sel_first=False) + carry[:, 0:1]
        carry = jnp.broadcast_to(fc[:, page - 1:page], carry.shape)
        fcums.append(fc)
    f_sc[...] = carry

    for h in range(nkv):
        qh = q_head(h)
        cs = slice(h * LANE, (h + 1) * LANE)
        s = jnp.concatenate(
            [lax.dot_general(qh, k_refs[i][0, :, cs].astype(BF16), NT, preferred_element_type=F32) * scale
             - bias_rows(fcums[i], h, page) for i in range(npg)], axis=1)

        def pv(pr, cs=cs):
            out = None
            for i in range(npg):
                t_ = jnp.dot(pr[:, i * page:(i + 1) * page].astype(BF16), v_refs[i][0, :, cs].astype(BF16),
                             preferred_element_type=F32)
                out = t_ if out is None else out + t_
            return out

        update(h, s, pv)

    @pl.when(p == pl.num_programs(1) - 1)
    def _():
        nn = kn_ref.shape[1]
        rr = lax.broadcasted_iota(jnp.int32, (LANE, LANE), 0)
        cc = lax.broadcasted_iota(jnp.int32, (LANE, LANE), 1)
        fnew = _dot_sel((rr <= cc).astype(BF16), lfn_ref[0], sel_first=False) + f_sc[...][:, 0:1]
        row = lax.broadcasted_iota(jnp.int32, (gt, nn), 0)
        col = lax.broadcasted_iota(jnp.int32, (gt, nn), 1)
        tq_ = row % ts
        keep = (col <= tq_) & (col < ts)
        for h in range(nkv):
            qh = q_head(h)
            cs = slice(h * LANE, (h + 1) * LANE)
            s = lax.dot_general(qh, kn_ref[0, :, cs], NT, preferred_element_type=F32) * scale
            s = jnp.where(keep, s - bias_rows(fnew, h, nn), NEG_INF)
            update(h, s, lambda pr, cs=cs: jnp.dot(pr.astype(BF16), vn_ref[0, :, cs], preferred_element_type=F32))
        for h in range(nkv):
            o = acc_sc[h * gt:(h + 1) * gt] / l_sc[h * gt:(h + 1) * gt]
            for i in range(g):
                o_ref[0, :, (h * g + i) * LANE:(h * g + i + 1) * LANE] = o[i * ts:(i + 1) * ts].astype(o_ref.dtype)


def _fox_sample(page_table, qb, knew, vnew, lfnew, kc, vc, lfc, nkv, g):
    bs, ts, wq = qb.shape
    npages = page_table.shape[1]
    npg = _pick(npages, 8, 1)
    page = kc.shape[1]
    wk = kc.shape[2]
    nh = lfc.shape[1]
    nn = knew.shape[1]

    def pspec(shape, i):
        return pl.BlockSpec(shape, lambda b, p, pt, i=i: (pt[b * npages + p * npg + i], 0, 0))

    in_specs = [pl.BlockSpec((1, ts, wq), lambda b, p, pt: (b, 0, 0)),
                pl.BlockSpec((1, nn, wk), lambda b, p, pt: (b, 0, 0)),
                pl.BlockSpec((1, nn, wk), lambda b, p, pt: (b, 0, 0)),
                pl.BlockSpec((1, nh, LANE), lambda b, p, pt: (b, 0, 0))]
    in_specs += [pspec((1, page, wk), i) for i in range(npg)]
    in_specs += [pspec((1, page, wk), i) for i in range(npg)]
    in_specs += [pspec((1, nh, page), i) for i in range(npg)]
    rows = nkv * g * ts
    grid_spec = pltpu.PrefetchScalarGridSpec(
        num_scalar_prefetch=1,
        grid=(bs, npages // npg),
        in_specs=in_specs,
        out_specs=pl.BlockSpec((1, ts, wq), lambda b, p, pt: (b, 0, 0)),
        scratch_shapes=[pltpu.VMEM((rows, 1), F32), pltpu.VMEM((rows, 1), F32), pltpu.VMEM((rows, LANE), F32),
                        pltpu.VMEM((nh, LANE), F32)],
    )
    return pl.pallas_call(
        functools.partial(_fox_sample_kernel, npg=npg, nkv=nkv, g=g, ts=ts, scale=LANE ** -0.5),
        grid_spec=grid_spec,
        out_shape=jax.ShapeDtypeStruct((bs, ts, wq), BF16),
        compiler_params=_params("parallel", "arbitrary"),
        name="fox_sample",
    )(page_table.reshape(-1), qb, knew, vnew, lfnew, *([kc] * npg), *([vc] * npg), *([lfc] * npg))


def _gdn_prep_kernel(halo_ref, x_ref, w_ref, o_ref, *, stride, taps, tiles_per_seq, zero_start, n_norm, n_q, scale):
    i, j = pl.program_id(0), pl.program_id(1)
    halo = halo_ref[...]
    if zero_start:
        halo = jnp.where(i % tiles_per_seq == 0, 0.0, halo)
    hh, tt = halo.shape[0], x_ref.shape[0]
    ext = jnp.concatenate([halo, x_ref[...]], axis=0)
    y = None
    for tap in range(taps):
        off = hh - (taps - 1 - tap) * stride
        term = ext[off:off + tt] * w_ref[tap:tap + 1, :]
        y = term if y is None else y + term
    y = _silu(y)
    is_norm = j < n_norm
    sc = jnp.where(j < n_q, scale, 1.0)
    for h in range(y.shape[1] // LANE):
        sl = slice(h * LANE, (h + 1) * LANE)
        yh = y[:, sl]
        nrm = yh * lax.rsqrt(jnp.sum(yh * yh, axis=-1, keepdims=True) + L2_EPS) * sc
        o_ref[:, sl] = jnp.where(is_norm, nrm, yh)


def _gdn_prep(x, col0, width, halo_arr, halo_is_x, conv_w, tt, tc, stride, tiles_per_seq, key_width, scale):
    m = x.shape[0]
    taps = conv_w.shape[0]
    cb = col0 // tc
    if halo_is_x:
        hh = SUBLANE
        halo_spec = pl.BlockSpec((hh, tc), lambda i, j: (jnp.maximum(i * (tt // hh) - 1, 0), cb + j))
        halo_in = x
    else:
        hh = halo_arr.shape[0]
        halo_spec = pl.BlockSpec((hh, tc), lambda i, j: (0, j))
        halo_in = halo_arr
    return pl.pallas_call(
        functools.partial(_gdn_prep_kernel, stride=stride, taps=taps, tiles_per_seq=tiles_per_seq,
                          zero_start=halo_is_x, n_norm=2 * key_width // tc, n_q=key_width // tc, scale=scale),
        grid=(m // tt, width // tc),
        in_specs=[halo_spec,
                  pl.BlockSpec((tt, tc), lambda i, j: (i, cb + j)),
                  pl.BlockSpec((taps, tc), lambda i, j: (0, j))],
        out_specs=pl.BlockSpec((tt, tc), lambda i, j: (i, j)),
        out_shape=jax.ShapeDtypeStruct((m, width), F32),
        compiler_params=_params("parallel", "parallel"),
        name="gdn_prep",
    )(halo_in, x, conv_w)


def _gdn_kernel(*refs, hb, c, use_s0):
    if use_s0:
        q_ref, k_ref, v_ref, z_ref, gb_ref, nw_ref, s0_ref, o_ref, so_ref, s_sc = refs
    else:
        q_ref, k_ref, v_ref, z_ref, gb_ref, nw_ref, o_ref, so_ref, s_sc = refs
    n = pl.program_id(2)

    @pl.when(n == 0)
    def _():
        s_sc[...] = s0_ref[0] if use_s0 else jnp.zeros(s_sc.shape, F32)

    heads = range(hb)
    split = lambda ref: jnp.stack([ref[:, h * LANE:(h + 1) * LANE] for h in heads])
    q, k, v = split(q_ref), split(k_ref), split(v_ref)
    gb = gb_ref[0]
    r = lax.broadcasted_iota(jnp.int32, (c, c), 0)
    cc = lax.broadcasted_iota(jnp.int32, (c, c), 1)
    incl, strict = r >= cc, r > cc
    gc_all = _dot_sel(incl.astype(BF16), gb)
    gc_t = gc_all.T
    gcol = jnp.stack([gc_all[:, h:h + 1] for h in heads])
    grow = jnp.stack([gc_t[h:h + 1, :] for h in heads])
    beta = jnp.stack([gb[:, hb + h:hb + h + 1] for h in heads])
    glast = gcol[:, c - 1:c, :]
    decay = jnp.exp(jnp.where(incl[None], gcol - grow, NEG_INF))
    kb = k * beta
    bdot = lambda a, b, eq: jnp.einsum(eq, a.astype(BF16), b.astype(BF16), preferred_element_type=F32)

    def bdot3(a, b, eq):
        ah, al = _split2(a)
        bh, bl = _split2(b)
        e = functools.partial(jnp.einsum, eq, preferred_element_type=F32)
        return e(ah, bh) + (e(ah, bl) + e(al, bh))

    mm = jnp.where(strict[None], bdot(kb, k, 'hck,hsk->hcs') * decay, 0.0)
    eye = (r == cc).astype(F32)[None]
    base = min(c, SUBLANE)
    same = lambda s: ((r // s) == (cc // s))[None]
    mp = jnp.where(same(base), mm, 0.0)
    tinv = eye - mp
    span = 2
    while span < base:
        mp = bdot3(mp, mp, 'hcs,hsk->hck')
        tinv = bdot3(tinv, eye + mp, 'hcs,hsk->hck')
        span *= 2
    blk = base
    while blk < c:
        off = jnp.where(same(2 * blk) & ~same(blk), mm, 0.0)
        tinv = tinv - bdot3(tinv, bdot3(off, tinv, 'hcs,hsk->hck'), 'hcs,hsk->hck')
        blk *= 2
    egc = jnp.exp(gcol)
    u = bdot(tinv, v * beta, 'hcs,hsv->hcv')
    w = bdot(tinv, kb * egc, 'hcs,hsk->hck')
    a = bdot(q, k, 'hck,hsk->hcs') * decay
    qd = q * egc
    kd = k * jnp.exp(glast - gcol)
    s = s_sc[...]
    v_new = u - bdot(w, s, 'hck,hkv->hcv')
    o = bdot(qd, s, 'hck,hkv->hcv') + bdot(a, v_new, 'hcs,hsv->hcv')
    s_new = s * jnp.exp(glast) + bdot(kd, v_new, 'hck,hcv->hkv')
    s_sc[...] = s_new

    @pl.when(n == pl.num_programs(2) - 1)
    def _():
        so_ref[0] = s_new

    on = o * lax.rsqrt(jnp.mean(o * o, axis=-1, keepdims=True) + RMS_EPS) * nw_ref[...][None]
    for h in heads:
        sl = slice(h * LANE, (h + 1) * LANE)
        o_ref[:, sl] = (on[h] * _silu(z_ref[:, sl])).astype(o_ref.dtype)


def _gdn(qkv, z_arr, z_col0, gb, norm_w, s0, nb, tseq, c, hb, nheads):
    m = qkv.shape[0]
    n = tseq // c
    w = hb * LANE
    ng = nheads // hb
    kw = nheads * LANE
    row_col = lambda cb: pl.BlockSpec((c, w), lambda b, hg, i, cb=cb: (b * n + i, cb + hg))
    in_specs = [row_col(0), row_col(kw // w), row_col(2 * kw // w),
                pl.BlockSpec((c, w), lambda b, hg, i: (b * n + i, z_col0 // w + hg)),
                pl.BlockSpec((1, c, LANE), lambda b, hg, i: (hg, b * n + i, 0)),
                pl.BlockSpec((1, LANE), lambda b, hg, i: (0, 0))]
    args = [qkv, qkv, qkv, z_arr, gb, norm_w.reshape(1, LANE)]
    state_spec = pl.BlockSpec((1, hb, LANE, LANE), lambda b, hg, i: (b, hg, 0, 0))
    if s0 is not None:
        in_specs.append(state_spec)
        args.append(s0)
    return pl.pallas_call(
        functools.partial(_gdn_kernel, hb=hb, c=c, use_s0=s0 is not None),
        grid=(nb, ng, n),
        in_specs=in_specs,
        out_specs=[pl.BlockSpec((c, w), lambda b, hg, i: (b * n + i, hg)), state_spec],
        out_shape=[jax.ShapeDtypeStruct((m, kw), BF16), jax.ShapeDtypeStruct((nb, nheads, LANE, LANE), F32)],
        scratch_shapes=[pltpu.VMEM((hb, LANE, LANE), F32)],
        compiler_params=_params("parallel", "parallel", "arbitrary"),
        name="gdn",
    )(*args)


def _merge_kernel(of_ref, og_ref, wf_ref, wg_ref, gf_ref, gg_ref, o_ref):
    a = jnp.dot(of_ref[...], wf_ref[...], preferred_element_type=F32)
    b = jnp.dot(og_ref[...], wg_ref[...], preferred_element_type=F32)
    o_ref[...] = (jax.nn.sigmoid(gf_ref[...]) * a + jax.nn.sigmoid(gg_ref[...]) * b).astype(o_ref.dtype)


def _merge(o_fox, o_gdn, w_bf, w_bg, proj, gf_col0, gg_col0, tm, tn):
    m, kf = o_fox.shape
    kg = o_gdn.shape[1]
    n = w_bf.shape[1]
    return pl.pallas_call(
        _merge_kernel,
        grid=(m // tm, n // tn),
        in_specs=[pl.BlockSpec((tm, kf), lambda i, j: (i, 0)),
                  pl.BlockSpec((tm, kg), lambda i, j: (i, 0)),
                  pl.BlockSpec((kf, tn), lambda i, j: (0, j)),
                  pl.BlockSpec((kg, tn), lambda i, j: (0, j)),
                  pl.BlockSpec((tm, tn), lambda i, j: (i, gf_col0 // tn + j)),
                  pl.BlockSpec((tm, tn), lambda i, j: (i, gg_col0 // tn + j))],
        out_specs=pl.BlockSpec((tm, tn), lambda i, j: (i, j)),
        out_shape=jax.ShapeDtypeStruct((m, n), BF16),
        compiler_params=_params("parallel", "arbitrary"),
        name="merge",
    )(o_fox, o_gdn, w_bf, w_bg, proj, proj)


def _resid_kernel(a_ref, w_ref, x_ref, gt_ref, o_ref):
    acc = jnp.dot(a_ref[...], w_ref[...], preferred_element_type=F32)
    o_ref[0] = x_ref[0] + _rows(acc, gt_ref[0]) * acc


def _resid_matmul(a, w, x3, gt3, tiles_per_mod, tn, name):
    nt, tm, n = x3.shape
    k = a.shape[1]
    mr = gt3.shape[1]
    return pl.pallas_call(
        _resid_kernel,
        grid=(nt, n // tn),
        in_specs=[pl.BlockSpec((tm, k), lambda i, j: (i, 0)),
                  pl.BlockSpec((k, tn), lambda i, j: (0, j)),
                  pl.BlockSpec((1, tm, tn), lambda i, j: (i, 0, j)),
                  pl.BlockSpec((1, mr, tn), lambda i, j: (i // tiles_per_mod, 0, j))],
        out_specs=pl.BlockSpec((1, tm, tn), lambda i, j: (i, 0, j)),
        out_shape=jax.ShapeDtypeStruct((nt, tm, n), F32),
        compiler_params=_params("parallel", "arbitrary"),
        name=name,
    )(a, w, x3, gt3)


def _ffn_up_kernel(*refs, stride, taps, tiles_per_seq, halo_from_h):
    if halo_from_h:
        hh_ref, h_ref, wg_ref, wu_ref, cw_ref, act_ref, tail_ref, hcat = refs
    else:
        hist_ref, h_ref, wg_ref, wu_ref, cw_ref, act_ref, tail_ref = refs
    i, j = pl.program_id(0), pl.program_id(1)
    tm = h_ref.shape[0]
    if halo_from_h:
        hh = hh_ref.shape[0]

        @pl.when(j == 0)
        def _():
            hcat[0:hh, :] = jnp.where(i % tiles_per_seq == 0, jnp.zeros_like(hh_ref[...]), hh_ref[...])
            hcat[hh:hh + tm, :] = h_ref[...]

        ext = jnp.dot(hcat[...], wg_ref[...], preferred_element_type=F32)
    else:
        hh = hist_ref.shape[0]
        ext = jnp.concatenate([hist_ref[...], jnp.dot(h_ref[...], wg_ref[...], preferred_element_type=F32)], axis=0)
    up = jnp.dot(h_ref[...], wu_ref[...], preferred_element_type=F32)
    y = None
    for tap in range(taps):
        off = hh - (taps - 1 - tap) * stride
        term = ext[off:off + tm] * cw_ref[tap:tap + 1, :]
        y = term if y is None else y + term
    act_ref[...] = (_silu(y) * up).astype(act_ref.dtype)
    tail_ref[0] = ext[hh + tm - tail_ref.shape[1]:hh + tm]


def _ffn_up(h, hist, w_gate, w_up, conv_w, tm, tn, stride, tiles_per_seq, tail_rows):
    m, k = h.shape
    n = w_gate.shape[1]
    taps = conv_w.shape[0]
    halo_from_h = hist is None
    if halo_from_h:
        hh = BF16_ROWS
        first = pl.BlockSpec((hh, k), lambda i, j: (jnp.maximum(i * (tm // hh) - 1, 0), 0))
        first_arg = h
        scratch = [pltpu.VMEM((hh + tm, k), BF16)]
    else:
        hh = hist.shape[0]
        first = pl.BlockSpec((hh, tn), lambda i, j: (0, j))
        first_arg = hist
        scratch = []
    return pl.pallas_call(
        functools.partial(_ffn_up_kernel, stride=stride, taps=taps, tiles_per_seq=tiles_per_seq,
                          halo_from_h=halo_from_h),
        grid=(m // tm, n // tn),
        in_specs=[first,
                  pl.BlockSpec((tm, k), lambda i, j: (i, 0)),
                  pl.BlockSpec((k, tn), lambda i, j: (0, j)),
                  pl.BlockSpec((k, tn), lambda i, j: (0, j)),
                  pl.BlockSpec((taps, tn), lambda i, j: (0, j))],
        out_specs=[pl.BlockSpec((tm, tn), lambda i, j: (i, j)),
                   pl.BlockSpec((1, tail_rows, tn), lambda i, j: (i, 0, j))],
        out_shape=[jax.ShapeDtypeStruct((m, n), BF16), jax.ShapeDtypeStruct((m // tm, tail_rows, n), F32)],
        scratch_shapes=scratch,
        compiler_params=_params("parallel", "arbitrary"),
        name="ffn_up",
    )(first_arg, h, w_gate, w_up, conv_w)


def _layer(x2, mods, wts, lay, *, prompt, nb, tseq, fox_fn, gdn_s0, gdn_hist, ffn_hist):
    m, d = x2.shape
    sh_m, sc_m, gt_m, sh_f, sc_f, gt_f = mods
    nh, nkv, g, gh = lay["nh"], lay["nkv"], lay["g"], lay["gh"]
    kw = gh * LANE
    if prompt:
        tr = _pick(tseq, 256, SUBLANE)
        tm = _pick(tseq, 1024, LANE)
        stride = 1
    else:
        tr = nb
        tm = m
        stride = nb
    tiles_mod = (tseq // tr) if prompt else m // tr
    h = _normmod(x2.reshape(m // tr, tr, d), wts["norm_mix"], sc_m, sh_m, tiles_mod).reshape(m, d)
    proj = _matmul(h, wts["w_main"], tm, _pick(wts["w_main"].shape[1], 512, LANE))
    small = _small_heads(h, wts["w_small"], wts["p_small"], nh, tm)
    c0 = lay["cols"]
    qn, kn, vb, k_f, v_f = _qknorm(proj, wts["fox_q_norm"], wts["fox_k_norm"], nh, nkv, _pick(m, 256, SUBLANE))
    o_fox = fox_fn(qn, kn, vb, small)
    cw = 3 * kw
    tc = _pick(kw, 512, LANE)
    if prompt:
        tt = _pick(tseq, 512, SUBLANE)
        qkv = _gdn_prep(proj, c0["gqkv"], cw, None, True, wts["gdn_conv_w"], tt, tc, 1, tseq // tt, kw, LANE ** -0.5)
        hb = lay["hb"]
        gbp = _group_gb(small, nh, gh, hb)
        o_gdn, s_fin = _gdn(qkv, proj, c0["gz"], gbp, wts["gdn_norm"], None, nb, tseq, lay["chunk"], hb, gh)
    else:
        qkv = _gdn_prep(proj, c0["gqkv"], cw, gdn_hist, False, wts["gdn_conv_w"], m, tc, nb, 1, kw, LANE ** -0.5)
        ts = m // nb
        cpad = lay["cpad"]
        to_bm = lambda a: jnp.pad(a.reshape(ts, nb, -1).transpose(1, 0, 2),
                                  ((0, 0), (0, cpad - ts), (0, 0))).reshape(nb * cpad, -1)
        hb = lay["hb"]
        gbs = _group_gb(to_bm(small), nh, gh, hb)
        zs = to_bm(proj[:, c0["gz"]:c0["gz"] + kw])
        o_bm, s_fin = _gdn(to_bm(qkv), zs, 0, gbs, wts["gdn_norm"], gdn_s0, nb, cpad, cpad, hb, gh)
        o_gdn = o_bm.reshape(nb, cpad, kw)[:, :ts].transpose(1, 0, 2).reshape(m, kw)
    tn = _pick(d, 512, LANE)
    merged = _merge(o_fox, o_gdn, wts["w_branch_fox"], wts["w_branch_gdn"], proj, c0["gate_fox"], c0["gate_gdn"],
                    tm, tn)
    tiles_mod_m = (tseq // tm) if prompt else 1
    x3 = _resid_matmul(merged, wts["w_out"], x2.reshape(m // tm, tm, d), gt_m, tiles_mod_m, tn, "out_proj")
    x2 = x3.reshape(m, d)
    h2 = _normmod(x2.reshape(m // tr, tr, d), wts["norm_ffn"], sc_f, sh_f, tiles_mod).reshape(m, d)
    dff = wts["ffn_w_gate"].shape[1]
    tnf = _pick(dff, 256, LANE)
    act, tail = _ffn_up(h2, ffn_hist, wts["ffn_w_gate"], wts["ffn_w_up"], wts["ffn_conv_w"], tm, tnf, stride,
                        tseq // tm if prompt else 1, SUBLANE if prompt else m)
    tm2 = _pick(tm, 512, LANE) if prompt else m
    y3 = _resid_matmul(act, wts["ffn_w_down"], x2.reshape(m // tm2, tm2, d), gt_f,
                       (tseq // tm2) if prompt else 1, _pick(d, 256, LANE), "ffn_down")
    return y3.reshape(m, d), dict(proj=proj, small=small, k=k_f, v=v_f, s=s_fin, tail=tail)


def _group_gb(small, nh, gh, hb):
    rows = small.shape[0]
    gg = small[:, nh:nh + gh].reshape(rows, gh // hb, hb)
    bb = small[:, nh + gh:nh + 2 * gh].reshape(rows, gh // hb, hb)
    gb = jnp.concatenate([gg, bb], axis=-1).transpose(1, 0, 2)
    return jnp.pad(gb, ((0, 0), (0, 0), (0, LANE - 2 * hb)))


def kernel(x_prompt, x_sample, cache_k, cache_v, cache_logf, state_gdn, state_gdn_conv, state_ffn_conv, page_table, c_prompt, c_sample, w_ada, b_ada, norm_mix, norm_ffn, w_in, fox_b_f, fox_q_norm, fox_k_norm, gdn_conv_w, gdn_A_log, gdn_dt_bias, gdn_norm, w_branch_fox, w_branch_gdn, w_out, ffn_w_gate, ffn_w_up, ffn_conv_w, ffn_w_down):
    nb, t, d = x_prompt.shape
    bs, ts, _ = x_sample.shape
    depth = w_in.shape[0]
    page, nkv = cache_k.shape[2], cache_k.shape[3]
    nh = cache_logf.shape[-1]
    g = nh // nkv
    gh = state_gdn.shape[2]
    kw = gh * LANE
    gconv = gdn_conv_w.shape[1]
    fconv = ffn_conv_w.shape[1]
    dff = ffn_w_gate.shape[-1]
    assert cache_k.shape[-1] == LANE and state_gdn.shape[-1] == LANE and state_gdn.shape[-2] == LANE
    assert gh == nh and ts >= gconv - 1 and ts >= fconv - 1

    sizes = (nh * LANE, nkv * LANE, nkv * LANE, nh, 3 * kw, kw, gh, gh, d, d)
    starts = [0]
    for s_ in sizes:
        starts.append(starts[-1] + s_)
    assert starts[-1] == w_in.shape[-1]
    big = (0, 1, 2, 4, 5, 8, 9)
    names = ("fq", "fk", "fv", "gqkv", "gz", "gate_fox", "gate_gdn")
    cols, off = {}, 0
    for nm, i in zip(names, big):
        cols[nm] = off
        off += sizes[i]
    hb = 4 if gh % 4 == 0 else gh
    lay = dict(nh=nh, nkv=nkv, g=g, gh=gh, cols=cols, hb=hb, chunk=_pick(t, 64, SUBLANE), cpad=SUBLANE)

    xs = x_sample.transpose(1, 0, 2).reshape(ts * bs, d)
    xp = x_prompt.reshape(nb * t, d)
    mc = -(-nb // SUBLANE) * SUBLANE
    c_all = jnp.concatenate([c_prompt, jnp.zeros((mc - nb, d), F32), c_sample], axis=0)

    outs_p = [[] for _ in range(6)]
    outs_s = [[] for _ in range(6)]
    for l in range(depth):
        wi = w_in[l]
        w_main = jnp.concatenate([wi[:, starts[i]:starts[i + 1]] for i in big], axis=1).astype(BF16)
        w_small = jnp.concatenate([wi[:, starts[3]:starts[4]], wi[:, starts[6]:starts[7]], wi[:, starts[7]:starts[8]]],
                                  axis=1)
        w_small = jnp.pad(w_small, ((0, 0), (0, LANE - 3 * nh))).astype(BF16)
        zpad = jnp.zeros((LANE - 2 * nh,), F32)
        p_small = jnp.stack([jnp.concatenate([fox_b_f[l], gdn_dt_bias[l], zpad]),
                             jnp.concatenate([jnp.zeros((nh,), F32), gdn_A_log[l], zpad])])
        p_small = jnp.pad(p_small, ((0, SUBLANE - 2), (0, 0)))
        wts = dict(norm_mix=norm_mix[l], norm_ffn=norm_ffn[l], w_main=w_main, w_small=w_small, p_small=p_small,
                   fox_q_norm=fox_q_norm[l], fox_k_norm=fox_k_norm[l], gdn_conv_w=gdn_conv_w[l],
                   gdn_norm=gdn_norm[l], w_branch_fox=w_branch_fox[l].astype(BF16),
                   w_branch_gdn=w_branch_gdn[l].astype(BF16), w_out=w_out[l].astype(BF16),
                   ffn_w_gate=ffn_w_gate[l].astype(BF16), ffn_w_up=ffn_w_up[l].astype(BF16),
                   ffn_conv_w=ffn_conv_w[l], ffn_w_down=ffn_w_down[l].astype(BF16))

        mod = _ada(c_all, w_ada[l], b_ada[l])
        mods_p = [mod[:nb, i * d:(i + 1) * d].reshape(nb, 1, d) for i in range(6)]
        mods_s = [mod[mc:, i * d:(i + 1) * d].reshape(1, bs, d) for i in range(6)]

        def fox_p(qn, kn, vb, small):
            f_hm = _cumf(small, nb, t)
            fpad = jnp.pad(f_hm[:, :nh].reshape(nb, nkv, g, t), ((0, 0), (0, 0), (0, SUBLANE - g), (0, 0)))
            return _fox_prompt(qn, kn, vb, fpad, nb, t, nkv, g)

        xp, st = _layer(xp, mods_p, wts, lay, prompt=True, nb=nb, tseq=t, fox_fn=fox_p, gdn_s0=None,
                        gdn_hist=None, ffn_hist=None)
        proj3 = st["proj"].reshape(nb, t, -1)
        outs_p[0].append(st["k"].reshape(nb, t, nkv, LANE))
        outs_p[1].append(st["v"].reshape(nb, t, nkv, LANE))
        outs_p[2].append(st["small"].reshape(nb, t, LANE)[:, :, :nh])
        outs_p[3].append(st["s"])
        outs_p[4].append(proj3[:, t - (gconv - 1):, cols["gqkv"]:cols["gqkv"] + 3 * kw])
        tail = st["tail"].reshape(nb, -1, SUBLANE, dff)
        outs_p[5].append(tail[:, -1, SUBLANE - (fconv - 1):, :])

        kc = cache_k[l].reshape(-1, page, nkv * LANE)
        vc = cache_v[l].reshape(-1, page, nkv * LANE)
        lfc = cache_logf[l].transpose(0, 2, 1)

        def fox_s(qn, kn, vb, small):
            bm = lambda a: a.reshape(ts, bs, -1).transpose(1, 0, 2)
            padr = lambda a: jnp.pad(a, ((0, 0), (0, BF16_ROWS - ts), (0, 0)))
            lfn = jnp.pad(bm(small)[:, :, :nh].transpose(0, 2, 1), ((0, 0), (0, 0), (0, LANE - ts)))
            o = _fox_sample(page_table, bm(qn), padr(bm(kn)), padr(bm(vb)), lfn, kc, vc, lfc, nkv, g)
            return o.transpose(1, 0, 2).reshape(ts * bs, -1)

        ghist = state_gdn_conv[l].transpose(1, 0, 2).reshape((gconv - 1) * bs, 3 * kw)
        fhist = state_ffn_conv[l].transpose(1, 0, 2).reshape((fconv - 1) * bs, dff)
        xs, st = _layer(xs, mods_s, wts, lay, prompt=False, nb=bs, tseq=ts, fox_fn=fox_s, gdn_s0=state_gdn[l],
                        gdn_hist=ghist, ffn_hist=fhist)
        bm3 = lambda a: a.reshape(ts, bs, -1).transpose(1, 0, 2)
        outs_s[0].append(bm3(st["k"]).reshape(bs, ts, nkv, LANE))
        outs_s[1].append(bm3(st["v"]).reshape(bs, ts, nkv, LANE))
        outs_s[2].append(bm3(st["small"])[:, :, :nh])
        outs_s[3].append(st["s"])
        graw = bm3(st["proj"][:, cols["gqkv"]:cols["gqkv"] + 3 * kw])
        outs_s[4].append(jnp.concatenate([state_gdn_conv[l], graw], axis=1)[:, -(gconv - 1):])
        fraw = bm3(st["tail"].reshape(ts * bs, dff))
        outs_s[5].append(jnp.concatenate([state_ffn_conv[l], fraw], axis=1)[:, -(fconv - 1):])

    y_prompt = xp.reshape(nb, t, d)
    y_sample = xs.reshape(ts, bs, d).transpose(1, 0, 2)
    return (y_prompt, y_sample, *(jnp.stack(a) for a in outs_p), *(jnp.stack(a) for a in outs_s))
```

```python
import functools

import jax
import jax.numpy as jnp
from jax import lax
from jax.experimental import pallas as pl
from jax.experimental.pallas import tpu as pltpu

F32 = jnp.float32
BF16 = jnp.bfloat16
RMS_EPS = 1e-6
L2_EPS = 1e-6
NEG_INF = -1e30
LANE = 128
SUBLANE = 8
BF16_ROWS = 16
VMEM_LIMIT_BYTES = 56 * 2 ** 20

NN = (((1,), (0,)), ((), ()))
NT = (((1,), (1,)), ((), ()))
TN = (((0,), (0,)), ((), ()))


def _pick(n, pref, align):
    t = min(pref, n) // align * align
    while t >= align:
        if n % t == 0:
            return t
        t -= align
    return n


def _params(*sem):
    return pltpu.CompilerParams(dimension_semantics=sem, vmem_limit_bytes=VMEM_LIMIT_BYTES)


def _dot(a, b, dims=NN):
    return lax.dot_general(a.astype(BF16), b.astype(BF16), dims, preferred_element_type=F32)


def _split2(a):
    hi = a.astype(BF16)
    lo = (a - hi.astype(F32)).astype(BF16)
    return hi, lo


def _dot3(a, b, dims=NN):
    ah, al = _split2(a)
    bh, bl = _split2(b)
    d = functools.partial(lax.dot_general, dimension_numbers=dims, preferred_element_type=F32)
    return d(ah, bh) + (d(ah, bl) + d(al, bh))


def _dot_sel(sel, x, sel_first=True, dims=NN):
    h = x.astype(BF16)
    r = x - h.astype(F32)
    m = r.astype(BF16)
    l = (r - m.astype(F32)).astype(BF16)
    if sel_first:
        d = lambda p: lax.dot_general(sel, p, dims, preferred_element_type=F32)
    else:
        d = lambda p: lax.dot_general(p, sel, dims, preferred_element_type=F32)
    return d(h) + (d(m) + d(l))


def _silu(x):
    return x * jax.nn.sigmoid(x)


def _softplus(x):
    return jnp.maximum(x, 0.0) + jnp.log1p(jnp.exp(-jnp.abs(x)))


def _rows(a, m):
    if m.shape[0] == 1 or m.shape[0] == a.shape[0]:
        return m
    reps = a.shape[0] // m.shape[0]
    return jnp.concatenate([m] * reps, axis=0)


def _ada_kernel(c_ref, w_ref, b_ref, o_ref):
    a = _silu(c_ref[...])
    o_ref[...] = _dot(a, w_ref[...]) + b_ref[...]


def _ada(c_all, w_ada, b_ada):
    mc, d = c_all.shape
    n = w_ada.shape[1]
    tn = _pick(n, 512, LANE)
    return pl.pallas_call(
        _ada_kernel,
        grid=(n // tn,),
        in_specs=[pl.BlockSpec((mc, d), lambda j: (0, 0)),
                  pl.BlockSpec((d, tn), lambda j: (0, j)),
                  pl.BlockSpec((1, tn), lambda j: (0, j))],
        out_specs=pl.BlockSpec((mc, tn), lambda j: (0, j)),
        out_shape=jax.ShapeDtypeStruct((mc, n), F32),
        compiler_params=_params("parallel"),
        name="ada",
    )(c_all, w_ada, b_ada.reshape(1, n))


def _normmod_kernel(x_ref, g_ref, sc_ref, sh_ref, o_ref):
    x = x_ref[0]
    y = x * lax.rsqrt(jnp.mean(x * x, axis=-1, keepdims=True) + RMS_EPS) * g_ref[...]
    o_ref[0] = (y * (1.0 + _rows(y, sc_ref[0])) + _rows(y, sh_ref[0])).astype(o_ref.dtype)


def _normmod(x3, gain, sc3, sh3, tiles_per_mod):
    nt, tr, d = x3.shape
    mr = sc3.shape[1]
    mod_spec = pl.BlockSpec((1, mr, d), lambda i: (i // tiles_per_mod, 0, 0))
    return pl.pallas_call(
        _normmod_kernel,
        grid=(nt,),
        in_specs=[pl.BlockSpec((1, tr, d), lambda i: (i, 0, 0)),
                  pl.BlockSpec((1, d), lambda i: (0, 0)),
                  mod_spec, mod_spec],
        out_specs=pl.BlockSpec((1, tr, d), lambda i: (i, 0, 0)),
        out_shape=jax.ShapeDtypeStruct((nt, tr, d), BF16),
        compiler_params=_params("parallel"),
        name="normmod",
    )(x3, gain.reshape(1, d), sc3, sh3)


def _mm_kernel(x_ref, w_ref, o_ref):
    o_ref[...] = jnp.dot(x_ref[...], w_ref[...], preferred_element_type=F32).astype(o_ref.dtype)


def _matmul(x, w, tm, tn, out_dtype=F32):
    m, k = x.shape
    n = w.shape[1]
    return pl.pallas_call(
        _mm_kernel,
        grid=(m // tm, n // tn),
        in_specs=[pl.BlockSpec((tm, k), lambda i, j: (i, 0)),
                  pl.BlockSpec((k, tn), lambda i, j: (0, j))],
        out_specs=pl.BlockSpec((tm, tn), lambda i, j: (i, j)),
        out_shape=jax.ShapeDtypeStruct((m, n), out_dtype),
        compiler_params=_params("parallel", "arbitrary"),
        name="inproj",
    )(x, w)


def _small_kernel(x_ref, w_ref, p_ref, o_ref, *, nh):
    acc = jnp.dot(x_ref[...], w_ref[...], preferred_element_type=F32)
    xb = acc + p_ref[0:1, :]
    lane = lax.broadcasted_iota(jnp.int32, acc.shape, 1)
    sp = _softplus(xb)
    logf = jnp.minimum(xb, 0.0) - jnp.log1p(jnp.exp(-jnp.abs(xb)))
    g = -jnp.exp(p_ref[1:2, :]) * sp
    beta = jax.nn.sigmoid(acc)
    o_ref[...] = jnp.where(lane < nh, logf, jnp.where(lane < 2 * nh, g, jnp.where(lane < 3 * nh, beta, 0.0)))


def _small_heads(h, w_small, p_small, nh, tm):
    m, k = h.shape
    return pl.pallas_call(
        functools.partial(_small_kernel, nh=nh),
        grid=(m // tm,),
        in_specs=[pl.BlockSpec((tm, k), lambda i: (i, 0)),
                  pl.BlockSpec((k, LANE), lambda i: (0, 0)),
                  pl.BlockSpec((SUBLANE, LANE), lambda i: (0, 0))],
        out_specs=pl.BlockSpec((tm, LANE), lambda i: (i, 0)),
        out_shape=jax.ShapeDtypeStruct((m, LANE), F32),
        compiler_params=_params("parallel"),
        name="small_heads",
    )(h, w_small, p_small)


def _qknorm_kernel(x_ref, qg_ref, kg_ref, qn_ref, kn_ref, vb_ref, kf_ref, vf_ref, *, nq, nkv):
    def norm(x, g):
        return x * lax.rsqrt(jnp.mean(x * x, axis=-1, keepdims=True) + RMS_EPS) * g

    for h in range(nq):
        sl = slice(h * LANE, (h + 1) * LANE)
        qn_ref[:, sl] = norm(x_ref[:, sl], qg_ref[...]).astype(BF16)
    for h in range(nkv):
        sl = slice(h * LANE, (h + 1) * LANE)
        kx = norm(x_ref[:, (nq + h) * LANE:(nq + h + 1) * LANE], kg_ref[...])
        kf_ref[:, sl] = kx
        kn_ref[:, sl] = kx.astype(BF16)
        vx = x_ref[:, (nq + nkv + h) * LANE:(nq + nkv + h + 1) * LANE]
        vf_ref[:, sl] = vx
        vb_ref[:, sl] = vx.astype(BF16)


def _qknorm(proj, q_gain, k_gain, nq, nkv, tm):
    m = proj.shape[0]
    wq, wk = nq * LANE, nkv * LANE
    row = lambda w: pl.BlockSpec((tm, w), lambda i: (i, 0))
    gain = pl.BlockSpec((1, LANE), lambda i: (0, 0))
    return pl.pallas_call(
        functools.partial(_qknorm_kernel, nq=nq, nkv=nkv),
        grid=(m // tm,),
        in_specs=[row(wq + 2 * wk), gain, gain],
        out_specs=[row(wq), row(wk), row(wk), row(wk), row(wk)],
        out_shape=[jax.ShapeDtypeStruct((m, wq), BF16), jax.ShapeDtypeStruct((m, wk), BF16),
                   jax.ShapeDtypeStruct((m, wk), BF16), jax.ShapeDtypeStruct((m, wk), F32),
                   jax.ShapeDtypeStruct((m, wk), F32)],
        compiler_params=_params("parallel"),
        name="qknorm",
    )(proj, q_gain.reshape(1, LANE), k_gain.reshape(1, LANE))


def _cumf_kernel(x_ref, o_ref, *, blk):
    t = x_ref.shape[0]
    r = lax.broadcasted_iota(jnp.int32, (blk, blk), 0)
    c = lax.broadcasted_iota(jnp.int32, (blk, blk), 1)
    tri = (r >= c).astype(BF16)
    carry = jnp.zeros((1, LANE), F32)
    for i in range(t // blk):
        cum = _dot_sel(tri, x_ref[i * blk:(i + 1) * blk, :]) + carry
        carry = cum[blk - 1:blk, :]
        o_ref[0, :, i * blk:(i + 1) * blk] = cum.T


def _cumf(small, nb, t):
    blk = _pick(t, 256, LANE)
    return pl.pallas_call(
        functools.partial(_cumf_kernel, blk=blk),
        grid=(nb,),
        in_specs=[pl.BlockSpec((t, LANE), lambda b: (b, 0))],
        out_specs=pl.BlockSpec((1, LANE, t), lambda b: (b, 0, 0)),
        out_shape=jax.ShapeDtypeStruct((nb, LANE, t), F32),
        compiler_params=_params("parallel"),
        name="cumf",
    )(small)


def _fox_prompt_kernel(q_ref, k_ref, v_ref, f_ref, o_ref, m_sc, l_sc, acc_sc, *, g, tq, scale):
    qi = pl.program_id(2)
    m_sc[...] = jnp.full(m_sc.shape, NEG_INF, F32)
    l_sc[...] = jnp.zeros(l_sc.shape, F32)
    acc_sc[...] = jnp.zeros(acc_sc.shape, F32)
    qs = jnp.concatenate([q_ref[:, i * LANE:(i + 1) * LANE] for i in range(g)], axis=0)

    def block(ki, masked):
        ks = pl.ds(pl.multiple_of(ki * tq, tq), tq)
        s = lax.dot_general(qs, k_ref[ks, :], NT, preferred_element_type=F32) * scale
        s = s - jnp.concatenate([jnp.broadcast_to(f_ref[0, 0, i:i + 1, ks], (tq, tq)) for i in range(g)], axis=0)
        if masked:
            keep = lax.broadcasted_iota(jnp.int32, (tq, tq), 1) <= lax.broadcasted_iota(jnp.int32, (tq, tq), 0)
            s = jnp.where(jnp.concatenate([keep] * g, axis=0), s, NEG_INF)
        m_prev = m_sc[...]
        m_new = jnp.maximum(m_prev, jnp.max(s, axis=-1, keepdims=True))
        alpha = jnp.exp(m_prev - m_new)
        p = jnp.exp(s - m_new)
        l_sc[...] = alpha * l_sc[...] + jnp.sum(p, axis=-1, keepdims=True)
        acc_sc[...] = alpha * acc_sc[...] + jnp.dot(p.astype(BF16), v_ref[ks, :], preferred_element_type=F32)
        m_sc[...] = m_new

    def body(ki, carry):
        block(ki, False)
        return carry

    lax.fori_loop(0, qi, body, 0)
    block(qi, True)
    o = acc_sc[...] / l_sc[...]
    for i in range(g):
        o_ref[:, i * LANE:(i + 1) * LANE] = o[i * tq:(i + 1) * tq].astype(o_ref.dtype)


def _fox_prompt(qn, kn, vb, fpad, nb, t, nkv, g):
    tq = _pick(t, 512, LANE)
    nq = t // tq
    kv_spec = pl.BlockSpec((t, LANE), lambda b, h, qi: (b, h))
    return pl.pallas_call(
        functools.partial(_fox_prompt_kernel, g=g, tq=tq, scale=LANE ** -0.5),
        grid=(nb, nkv, nq),
        in_specs=[pl.BlockSpec((tq, g * LANE), lambda b, h, qi: (b * nq + qi, h)),
                  kv_spec, kv_spec,
                  pl.BlockSpec((1, 1, SUBLANE, t), lambda b, h, qi: (b, h, 0, 0))],
        out_specs=pl.BlockSpec((tq, g * LANE), lambda b, h, qi: (b * nq + qi, h)),
        out_shape=jax.ShapeDtypeStruct((nb * t, nkv * g * LANE), BF16),
        scratch_shapes=[pltpu.VMEM((g * tq, 1), F32), pltpu.VMEM((g * tq, 1), F32),
                        pltpu.VMEM((g * tq, LANE), F32)],
        compiler_params=_params("parallel", "parallel", "arbitrary"),
        name="fox_prompt",
    )(qn, kn, vb, fpad)


def _fox_sample_kernel(pt_ref, q_ref, kn_ref, vn_ref, lfn_ref, *refs, npg, nkv, g, ts, scale):
    k_refs, v_refs, lf_refs = refs[:npg], refs[npg:2 * npg], refs[2 * npg:3 * npg]
    o_ref, m_sc, l_sc, acc_sc, f_sc = refs[3 * npg:]
    p = pl.program_id(1)
    page = k_refs[0].shape[1]
    nh = nkv * g
    rows = ts * nh
    cols = page * nkv

    @pl.when(p == 0)
    def _():
        m_sc[...] = jnp.full(m_sc.shape, NEG_INF, F32)
        l_sc[...] = jnp.zeros(l_sc.shape, F32)
        acc_sc[...] = jnp.zeros(acc_sc.shape, F32)
        f_sc[...] = jnp.zeros(f_sc.shape, F32)

    q = q_ref[0]

    def update(s, pv):
        m_prev = m_sc[...]
        m_new = jnp.maximum(m_prev, jnp.max(s, axis=-1, keepdims=True))
        alpha = jnp.exp(m_prev - m_new)
        pr = jnp.exp(s - m_new)
        l_sc[...] = alpha * l_sc[...] + jnp.sum(pr, axis=-1, keepdims=True)
        acc_sc[...] = alpha * acc_sc[...] + pv(pr.astype(BF16))
        m_sc[...] = m_new

    def own_kv(width):
        rq = lax.broadcasted_iota(jnp.int32, (rows, width), 0)
        cq = lax.broadcasted_iota(jnp.int32, (rows, width), 1)
        return rq, cq, (cq % nkv) == ((rq % nh) // g)

    ue = (lax.broadcasted_iota(jnp.int32, (page, cols), 0)
          <= lax.broadcasted_iota(jnp.int32, (page, cols), 1) // nkv).astype(BF16)
    _, _, own = own_kv(cols)
    carry = f_sc[...]
    parts = []
    for i in range(npg):
        fexp = _dot_sel(ue, lf_refs[i][0], sel_first=False, dims=TN) + carry[:, 0:1]
        carry = jnp.broadcast_to(fexp[:, cols - 1:cols], carry.shape)
        kf = k_refs[i][0].reshape(cols, LANE).astype(BF16)
        s = lax.dot_general(q, kf, NT, preferred_element_type=F32) * scale - jnp.concatenate([fexp] * ts, axis=0)
        parts.append(jnp.where(own, s, NEG_INF))
    f_sc[...] = carry

    def pv_pages(pr):
        out = None
        for i in range(npg):
            t_ = jnp.dot(pr[:, i * cols:(i + 1) * cols], v_refs[i][0].reshape(cols, LANE).astype(BF16),
                         preferred_element_type=F32)
            out = t_ if out is None else out + t_
        return out

    update(jnp.concatenate(parts, axis=1), pv_pages)

    @pl.when(p == pl.num_programs(1) - 1)
    def _():
        nn = kn_ref.shape[1]
        nl = lfn_ref.shape[1]
        uen = (lax.broadcasted_iota(jnp.int32, (nl, nn), 0)
               <= lax.broadcasted_iota(jnp.int32, (nl, nn), 1) // nkv).astype(BF16)
        fnew = _dot_sel(uen, lfn_ref[0], sel_first=False, dims=TN)[:nh] + f_sc[...][:, 0:1]
        rq, cq, own_n = own_kv(nn)
        keep = own_n & ((cq // nkv) <= (rq // nh)) & (cq < ts * nkv)
        s = lax.dot_general(q, kn_ref[0], NT, preferred_element_type=F32) * scale - jnp.concatenate([fnew] * ts, axis=0)
        update(jnp.where(keep, s, NEG_INF), lambda pr: jnp.dot(pr, vn_ref[0], preferred_element_type=F32))
        o_ref[0] = (acc_sc[...] / l_sc[...]).astype(o_ref.dtype)


def _fox_sample(page_ids, qb, knew, vnew, lfnew, kc, vc, lfc, nkv, g, ts):
    bs, rows, _ = qb.shape
    npages = page_ids.shape[1]
    npg = _pick(npages, 8, 1)
    page = kc.shape[1]
    nh = lfc.shape[2]
    nn = knew.shape[1]
    nl = lfnew.shape[1]

    def kv_page(i):
        return pl.BlockSpec((1, page, nkv, LANE), lambda b, p, pt, i=i: (pt[b * npages + p * npg + i], 0, 0, 0))

    def lf_page(i):
        return pl.BlockSpec((1, page, nh), lambda b, p, pt, i=i: (pt[b * npages + p * npg + i], 0, 0))

    per_b = lambda r: pl.BlockSpec((1, r, LANE), lambda b, p, pt: (b, 0, 0))
    in_specs = [per_b(rows), per_b(nn), per_b(nn), per_b(nl)]
    in_specs += [kv_page(i) for i in range(npg)] + [kv_page(i) for i in range(npg)] + [lf_page(i) for i in range(npg)]
    grid_spec = pltpu.PrefetchScalarGridSpec(
        num_scalar_prefetch=1,
        grid=(bs, npages // npg),
        in_specs=in_specs,
        out_specs=per_b(rows),
        scratch_shapes=[pltpu.VMEM((rows, 1), F32), pltpu.VMEM((rows, 1), F32), pltpu.VMEM((rows, LANE), F32),
                        pltpu.VMEM((nh, LANE), F32)],
    )
    return pl.pallas_call(
        functools.partial(_fox_sample_kernel, npg=npg, nkv=nkv, g=g, ts=ts, scale=LANE ** -0.5),
        grid_spec=grid_spec,
        out_shape=jax.ShapeDtypeStruct((bs, rows, LANE), BF16),
        compiler_params=_params("parallel", "arbitrary"),
        name="fox_sample",
    )(page_ids.reshape(-1), qb, knew, vnew, lfnew, *([kc] * npg), *([vc] * npg), *([lfc] * npg))


def _gdn_prep_kernel(halo_ref, x_ref, w_ref, o_ref, *, stride, taps, tiles_per_seq, zero_start, n_norm, n_q, scale):
    i, j = pl.program_id(0), pl.program_id(1)
    halo = halo_ref[...]
    if zero_start:
        halo = jnp.where(i % tiles_per_seq == 0, 0.0, halo)
    hh, tt = halo.shape[0], x_ref.shape[0]
    ext = jnp.concatenate([halo, x_ref[...]], axis=0)
    y = None
    for tap in range(taps):
        off = hh - (taps - 1 - tap) * stride
        term = ext[off:off + tt] * w_ref[tap:tap + 1, :]
        y = term if y is None else y + term
    y = _silu(y)
    is_norm = j < n_norm
    sc = jnp.where(j < n_q, scale, 1.0)
    for h in range(y.shape[1] // LANE):
        sl = slice(h * LANE, (h + 1) * LANE)
        yh = y[:, sl]
        nrm = yh * lax.rsqrt(jnp.sum(yh * yh, axis=-1, keepdims=True) + L2_EPS) * sc
        o_ref[:, sl] = jnp.where(is_norm, nrm, yh)


def _gdn_prep(x, col0, width, halo_arr, halo_is_x, conv_w, tt, tc, stride, tiles_per_seq, key_width, scale):
    m = x.shape[0]
    taps = conv_w.shape[0]
    cb = col0 // tc
    if halo_is_x:
        hh = SUBLANE
        halo_spec = pl.BlockSpec((hh, tc), lambda i, j: (jnp.maximum(i * (tt // hh) - 1, 0), cb + j))
        halo_in = x
    else:
        hh = halo_arr.shape[0]
        halo_spec = pl.BlockSpec((hh, tc), lambda i, j: (0, j))
        halo_in = halo_arr
    return pl.pallas_call(
        functools.partial(_gdn_prep_kernel, stride=stride, taps=taps, tiles_per_seq=tiles_per_seq,
                          zero_start=halo_is_x, n_norm=2 * key_width // tc, n_q=key_width // tc, scale=scale),
        grid=(m // tt, width // tc),
        in_specs=[halo_spec,
                  pl.BlockSpec((tt, tc), lambda i, j: (i, cb + j)),
                  pl.BlockSpec((taps, tc), lambda i, j: (0, j))],
        out_specs=pl.BlockSpec((tt, tc), lambda i, j: (i, j)),
        out_shape=jax.ShapeDtypeStruct((m, width), F32),
        compiler_params=_params("parallel", "parallel"),
        name="gdn_prep",
    )(halo_in, x, conv_w)


def _gdn_kernel(*refs, hb, c, use_s0):
    if use_s0:
        q_ref, k_ref, v_ref, z_ref, gb_ref, nw_ref, s0_ref, o_ref, so_ref, s_sc = refs
    else:
        q_ref, k_ref, v_ref, z_ref, gb_ref, nw_ref, o_ref, so_ref, s_sc = refs
    n = pl.program_id(2)

    @pl.when(n == 0)
    def _():
        s_sc[...] = s0_ref[0] if use_s0 else jnp.zeros(s_sc.shape, F32)

    heads = range(hb)
    split = lambda ref: jnp.stack([ref[:, h * LANE:(h + 1) * LANE] for h in heads])
    q, k, v = split(q_ref), split(k_ref), split(v_ref)
    gb = gb_ref[0]
    r = lax.broadcasted_iota(jnp.int32, (c, c), 0)
    cc = lax.broadcasted_iota(jnp.int32, (c, c), 1)
    incl, strict = r >= cc, r > cc
    gc_all = _dot_sel(incl.astype(BF16), gb)
    gc_t = gc_all.T
    gcol = jnp.stack([gc_all[:, h:h + 1] for h in heads])
    grow = jnp.stack([gc_t[h:h + 1, :] for h in heads])
    beta = jnp.stack([gb[:, hb + h:hb + h + 1] for h in heads])
    glast = gcol[:, c - 1:c, :]
    decay = jnp.exp(jnp.where(incl[None], gcol - grow, NEG_INF))
    kb = k * beta
    bdot = lambda a, b, eq: jnp.einsum(eq, a.astype(BF16), b.astype(BF16), preferred_element_type=F32)

    def bdot3(a, b, eq):
        ah, al = _split2(a)
        bh, bl = _split2(b)
        e = functools.partial(jnp.einsum, eq, preferred_element_type=F32)
        return e(ah, bh) + (e(ah, bl) + e(al, bh))

    mm = jnp.where(strict[None], bdot(kb, k, 'hck,hsk->hcs') * decay, 0.0)
    eye = (r == cc).astype(F32)[None]
    base = min(c, SUBLANE)
    same = lambda s: ((r // s) == (cc // s))[None]
    mp = jnp.where(same(base), mm, 0.0)
    tinv = eye - mp
    span = 2
    while span < base:
        mp = bdot3(mp, mp, 'hcs,hsk->hck')
        tinv = bdot3(tinv, eye + mp, 'hcs,hsk->hck')
        span *= 2
    blk = base
    while blk < c:
        off = jnp.where(same(2 * blk) & ~same(blk), mm, 0.0)
        tinv = tinv - bdot3(tinv, bdot3(off, tinv, 'hcs,hsk->hck'), 'hcs,hsk->hck')
        blk *= 2
    egc = jnp.exp(gcol)
    u = bdot(tinv, v * beta, 'hcs,hsv->hcv')
    w = bdot(tinv, kb * egc, 'hcs,hsk->hck')
    a = bdot(q, k, 'hck,hsk->hcs') * decay
    qd = q * egc
    kd = k * jnp.exp(glast - gcol)
    s = s_sc[...]
    v_new = u - bdot(w, s, 'hck,hkv->hcv')
    o = bdot(qd, s, 'hck,hkv->hcv') + bdot(a, v_new, 'hcs,hsv->hcv')
    s_new = s * jnp.exp(glast) + bdot(kd, v_new, 'hck,hcv->hkv')
    s_sc[...] = s_new

    @pl.when(n == pl.num_programs(2) - 1)
    def _():
        so_ref[0] = s_new

    on = o * lax.rsqrt(jnp.mean(o * o, axis=-1, keepdims=True) + RMS_EPS) * nw_ref[...][None]
    for h in heads:
        sl = slice(h * LANE, (h + 1) * LANE)
        o_ref[:, sl] = (on[h] * _silu(z_ref[:, sl])).astype(o_ref.dtype)


def _gdn(qkv, z_arr, z_col0, gb, norm_w, s0, nb, tseq, c, hb, nheads):
    m = qkv.shape[0]
    n = tseq // c
    w = hb * LANE
    ng = nheads // hb
    kw = nheads * LANE
    row_col = lambda cb: pl.BlockSpec((c, w), lambda b, hg, i, cb=cb: (b * n + i, cb + hg))
    in_specs = [row_col(0), row_col(kw // w), row_col(2 * kw // w),
                pl.BlockSpec((c, w), lambda b, hg, i: (b * n + i, z_col0 // w + hg)),
                pl.BlockSpec((1, c, LANE), lambda b, hg, i: (hg, b * n + i, 0)),
                pl.BlockSpec((1, LANE), lambda b, hg, i: (0, 0))]
    args = [qkv, qkv, qkv, z_arr, gb, norm_w.reshape(1, LANE)]
    state_spec = pl.BlockSpec((1, hb, LANE, LANE), lambda b, hg, i: (b, hg, 0, 0))
    if s0 is not None:
        in_specs.append(state_spec)
        args.append(s0)
    return pl.pallas_call(
        functools.partial(_gdn_kernel, hb=hb, c=c, use_s0=s0 is not None),
        grid=(nb, ng, n),
        in_specs=in_specs,
        out_specs=[pl.BlockSpec((c, w), lambda b, hg, i: (b * n + i, hg)), state_spec],
        out_shape=[jax.ShapeDtypeStruct((m, kw), BF16), jax.ShapeDtypeStruct((nb, nheads, LANE, LANE), F32)],
        scratch_shapes=[pltpu.VMEM((hb, LANE, LANE), F32)],
        compiler_params=_params("parallel", "parallel", "arbitrary"),
        name="gdn",
    )(*args)


def _merge_kernel(of_ref, og_ref, wf_ref, wg_ref, gf_ref, gg_ref, o_ref):
    a = jnp.dot(of_ref[...], wf_ref[...], preferred_element_type=F32)
    b = jnp.dot(og_ref[...], wg_ref[...], preferred_element_type=F32)
    o_ref[...] = (jax.nn.sigmoid(gf_ref[...]) * a + jax.nn.sigmoid(gg_ref[...]) * b).astype(o_ref.dtype)


def _merge(o_fox, o_gdn, w_bf, w_bg, proj, gf_col0, gg_col0, tm, tn):
    m, kf = o_fox.shape
    kg = o_gdn.shape[1]
    n = w_bf.shape[1]
    return pl.pallas_call(
        _merge_kernel,
        grid=(m // tm, n // tn),
        in_specs=[pl.BlockSpec((tm, kf), lambda i, j: (i, 0)),
                  pl.BlockSpec((tm, kg), lambda i, j: (i, 0)),
                  pl.BlockSpec((kf, tn), lambda i, j: (0, j)),
                  pl.BlockSpec((kg, tn), lambda i, j: (0, j)),
                  pl.BlockSpec((tm, tn), lambda i, j: (i, gf_col0 // tn + j)),
                  pl.BlockSpec((tm, tn), lambda i, j: (i, gg_col0 // tn + j))],
        out_specs=pl.BlockSpec((tm, tn), lambda i, j: (i, j)),
        out_shape=jax.ShapeDtypeStruct((m, n), BF16),
        compiler_params=_params("parallel", "arbitrary"),
        name="merge",
    )(o_fox, o_gdn, w_bf, w_bg, proj, proj)


def _resid_kernel(a_ref, w_ref, x_ref, gt_ref, o_ref):
    acc = jnp.dot(a_ref[...], w_ref[...], preferred_element_type=F32)
    o_ref[0] = x_ref[0] + _rows(acc, gt_ref[0]) * acc


def _resid_matmul(a, w, x3, gt3, tiles_per_mod, tn, name):
    nt, tm, n = x3.shape
    k = a.shape[1]
    mr = gt3.shape[1]
    return pl.pallas_call(
        _resid_kernel,
        grid=(nt, n // tn),
        in_specs=[pl.BlockSpec((tm, k), lambda i, j: (i, 0)),
                  pl.BlockSpec((k, tn), lambda i, j: (0, j)),
                  pl.BlockSpec((1, tm, tn), lambda i, j: (i, 0, j)),
                  pl.BlockSpec((1, mr, tn), lambda i, j: (i // tiles_per_mod, 0, j))],
        out_specs=pl.BlockSpec((1, tm, tn), lambda i, j: (i, 0, j)),
        out_shape=jax.ShapeDtypeStruct((nt, tm, n), F32),
        compiler_params=_params("parallel", "arbitrary"),
        name=name,
    )(a, w, x3, gt3)


def _ffn_up_kernel(*refs, stride, taps, tiles_per_seq, halo_from_h):
    if halo_from_h:
        hh_ref, h_ref, wg_ref, wu_ref, cw_ref, act_ref, tail_ref, hcat = refs
    else:
        hist_ref, h_ref, wg_ref, wu_ref, cw_ref, act_ref, tail_ref = refs
    i, j = pl.program_id(0), pl.program_id(1)
    tm = h_ref.shape[0]
    if halo_from_h:
        hh = hh_ref.shape[0]

        @pl.when(j == 0)
        def _():
            hcat[0:hh, :] = jnp.where(i % tiles_per_seq == 0, jnp.zeros_like(hh_ref[...]), hh_ref[...])
            hcat[hh:hh + tm, :] = h_ref[...]

        ext = jnp.dot(hcat[...], wg_ref[...], preferred_element_type=F32)
    else:
        hh = hist_ref.shape[0]
        ext = jnp.concatenate([hist_ref[...], jnp.dot(h_ref[...], wg_ref[...], preferred_element_type=F32)], axis=0)
    up = jnp.dot(h_ref[...], wu_ref[...], preferred_element_type=F32)
    y = None
    for tap in range(taps):
        off = hh - (taps - 1 - tap) * stride
        term = ext[off:off + tm] * cw_ref[tap:tap + 1, :]
        y = term if y is None else y + term
    act_ref[...] = (_silu(y) * up).astype(act_ref.dtype)
    tail_ref[0] = ext[hh + tm - tail_ref.shape[1]:hh + tm]


def _ffn_up(h, hist, w_gate, w_up, conv_w, tm, tn, stride, tiles_per_seq, tail_rows):
    m, k = h.shape
    n = w_gate.shape[1]
    taps = conv_w.shape[0]
    halo_from_h = hist is None
    if halo_from_h:
        hh = BF16_ROWS
        first = pl.BlockSpec((hh, k), lambda i, j: (jnp.maximum(i * (tm // hh) - 1, 0), 0))
        first_arg = h
        scratch = [pltpu.VMEM((hh + tm, k), BF16)]
    else:
        hh = hist.shape[0]
        first = pl.BlockSpec((hh, tn), lambda i, j: (0, j))
        first_arg = hist
        scratch = []
    return pl.pallas_call(
        functools.partial(_ffn_up_kernel, stride=stride, taps=taps, tiles_per_seq=tiles_per_seq,
                          halo_from_h=halo_from_h),
        grid=(m // tm, n // tn),
        in_specs=[first,
                  pl.BlockSpec((tm, k), lambda i, j: (i, 0)),
                  pl.BlockSpec((k, tn), lambda i, j: (0, j)),
                  pl.BlockSpec((k, tn), lambda i, j: (0, j)),
                  pl.BlockSpec((taps, tn), lambda i, j: (0, j))],
        out_specs=[pl.BlockSpec((tm, tn), lambda i, j: (i, j)),
                   pl.BlockSpec((1, tail_rows, tn), lambda i, j: (i, 0, j))],
        out_shape=[jax.ShapeDtypeStruct((m, n), BF16), jax.ShapeDtypeStruct((m // tm, tail_rows, n), F32)],
        scratch_shapes=scratch,
        compiler_params=_params("parallel", "arbitrary"),
        name="ffn_up",
    )(first_arg, h, w_gate, w_up, conv_w)


def _layer(x2, mods, wts, lay, *, prompt, nb, tseq, fox_fn, gdn_s0, gdn_hist, ffn_hist):
    m, d = x2.shape
    sh_m, sc_m, gt_m, sh_f, sc_f, gt_f = mods
    nh, nkv, g, gh = lay["nh"], lay["nkv"], lay["g"], lay["gh"]
    kw = gh * LANE
    if prompt:
        tr = _pick(tseq, 256, SUBLANE)
        tm = _pick(tseq, 1024, LANE)
        stride = 1
    else:
        tr = nb
        tm = m
        stride = nb
    tiles_mod = (tseq // tr) if prompt else m // tr
    h = _normmod(x2.reshape(m // tr, tr, d), wts["norm_mix"], sc_m, sh_m, tiles_mod).reshape(m, d)
    proj = _matmul(h, wts["w_main"], tm, _pick(wts["w_main"].shape[1], 512, LANE))
    small = _small_heads(h, wts["w_small"], wts["p_small"], nh, tm)
    c0 = lay["cols"]
    qn, kn, vb, k_f, v_f = _qknorm(proj, wts["fox_q_norm"], wts["fox_k_norm"], nh, nkv, _pick(m, 256, SUBLANE))
    o_fox = fox_fn(qn, kn, vb, small)
    cw = 3 * kw
    tc = _pick(kw, 512, LANE)
    if prompt:
        tt = _pick(tseq, 512, SUBLANE)
        qkv = _gdn_prep(proj, c0["gqkv"], cw, None, True, wts["gdn_conv_w"], tt, tc, 1, tseq // tt, kw, LANE ** -0.5)
        hb = lay["hb"]
        gbp = _group_gb(small, nh, gh, hb)
        o_gdn, s_fin = _gdn(qkv, proj, c0["gz"], gbp, wts["gdn_norm"], None, nb, tseq, lay["chunk"], hb, gh)
    else:
        qkv = _gdn_prep(proj, c0["gqkv"], cw, gdn_hist, False, wts["gdn_conv_w"], m, tc, nb, 1, kw, LANE ** -0.5)
        ts = m // nb
        cpad = lay["cpad"]
        to_bm = lambda a: jnp.pad(a.reshape(ts, nb, -1).transpose(1, 0, 2),
                                  ((0, 0), (0, cpad - ts), (0, 0))).reshape(nb * cpad, -1)
        hb = lay["hb"]
        gbs = _group_gb(to_bm(small), nh, gh, hb)
        zs = to_bm(proj[:, c0["gz"]:c0["gz"] + kw])
        o_bm, s_fin = _gdn(to_bm(qkv), zs, 0, gbs, wts["gdn_norm"], gdn_s0, nb, cpad, cpad, hb, gh)
        o_gdn = o_bm.reshape(nb, cpad, kw)[:, :ts].transpose(1, 0, 2).reshape(m, kw)
    tn = _pick(d, 512, LANE)
    merged = _merge(o_fox, o_gdn, wts["w_branch_fox"], wts["w_branch_gdn"], proj, c0["gate_fox"], c0["gate_gdn"],
                    tm, tn)
    tiles_mod_m = (tseq // tm) if prompt else 1
    x3 = _resid_matmul(merged, wts["w_out"], x2.reshape(m // tm, tm, d), gt_m, tiles_mod_m, tn, "out_proj")
    x2 = x3.reshape(m, d)
    h2 = _normmod(x2.reshape(m // tr, tr, d), wts["norm_ffn"], sc_f, sh_f, tiles_mod).reshape(m, d)
    dff = wts["ffn_w_gate"].shape[1]
    tnf = _pick(dff, 256, LANE)
    act, tail = _ffn_up(h2, ffn_hist, wts["ffn_w_gate"], wts["ffn_w_up"], wts["ffn_conv_w"], tm, tnf, stride,
                        tseq // tm if prompt else 1, SUBLANE if prompt else m)
    tm2 = _pick(tm, 512, LANE) if prompt else m
    y3 = _resid_matmul(act, wts["ffn_w_down"], x2.reshape(m // tm2, tm2, d), gt_f,
                       (tseq // tm2) if prompt else 1, _pick(d, 256, LANE), "ffn_down")
    return y3.reshape(m, d), dict(proj=proj, small=small, k=k_f, v=v_f, s=s_fin, tail=tail)


def _group_gb(small, nh, gh, hb):
    rows = small.shape[0]
    gg = small[:, nh:nh + gh].reshape(rows, gh // hb, hb)
    bb = small[:, nh + gh:nh + 2 * gh].reshape(rows, gh // hb, hb)
    gb = jnp.concatenate([gg, bb], axis=-1).transpose(1, 0, 2)
    return jnp.pad(gb, ((0, 0), (0, 0), (0, LANE - 2 * hb)))


def kernel(x_prompt, x_sample, cache_k, cache_v, cache_logf, state_gdn, state_gdn_conv, state_ffn_conv, page_table, c_prompt, c_sample, w_ada, b_ada, norm_mix, norm_ffn, w_in, fox_b_f, fox_q_norm, fox_k_norm, gdn_conv_w, gdn_A_log, gdn_dt_bias, gdn_norm, w_branch_fox, w_branch_gdn, w_out, ffn_w_gate, ffn_w_up, ffn_conv_w, ffn_w_down):
    nb, t, d = x_prompt.shape
    bs, ts, _ = x_sample.shape
    depth = w_in.shape[0]
    page, nkv = cache_k.shape[2], cache_k.shape[3]
    nh = cache_logf.shape[-1]
    g = nh // nkv
    gh = state_gdn.shape[2]
    kw = gh * LANE
    gconv = gdn_conv_w.shape[1]
    fconv = ffn_conv_w.shape[1]
    dff = ffn_w_gate.shape[-1]
    assert cache_k.shape[-1] == LANE and state_gdn.shape[-1] == LANE and state_gdn.shape[-2] == LANE
    assert gh == nh and ts >= gconv - 1 and ts >= fconv - 1

    sizes = (nh * LANE, nkv * LANE, nkv * LANE, nh, 3 * kw, kw, gh, gh, d, d)
    starts = [0]
    for s_ in sizes:
        starts.append(starts[-1] + s_)
    assert starts[-1] == w_in.shape[-1]
    big = (0, 1, 2, 4, 5, 8, 9)
    names = ("fq", "fk", "fv", "gqkv", "gz", "gate_fox", "gate_gdn")
    cols, off = {}, 0
    for nm, i in zip(names, big):
        cols[nm] = off
        off += sizes[i]
    hb = _pick(gh, 16, 1)
    lay = dict(nh=nh, nkv=nkv, g=g, gh=gh, cols=cols, hb=hb, chunk=_pick(t, 64, SUBLANE), cpad=SUBLANE)

    xs = x_sample.transpose(1, 0, 2).reshape(ts * bs, d)
    xp = x_prompt.reshape(nb * t, d)
    mc = -(-nb // SUBLANE) * SUBLANE
    c_all = jnp.concatenate([c_prompt, jnp.zeros((mc - nb, d), F32), c_sample], axis=0)

    outs_p = [[] for _ in range(6)]
    outs_s = [[] for _ in range(6)]
    for l in range(depth):
        wi = w_in[l]
        w_main = jnp.concatenate([wi[:, starts[i]:starts[i + 1]] for i in big], axis=1).astype(BF16)
        w_small = jnp.concatenate([wi[:, starts[3]:starts[4]], wi[:, starts[6]:starts[7]], wi[:, starts[7]:starts[8]]],
                                  axis=1)
        w_small = jnp.pad(w_small, ((0, 0), (0, LANE - 3 * nh))).astype(BF16)
        zpad = jnp.zeros((LANE - 2 * nh,), F32)
        p_small = jnp.stack([jnp.concatenate([fox_b_f[l], gdn_dt_bias[l], zpad]),
                             jnp.concatenate([jnp.zeros((nh,), F32), gdn_A_log[l], zpad])])
        p_small = jnp.pad(p_small, ((0, SUBLANE - 2), (0, 0)))
        wts = dict(norm_mix=norm_mix[l], norm_ffn=norm_ffn[l], w_main=w_main, w_small=w_small, p_small=p_small,
                   fox_q_norm=fox_q_norm[l], fox_k_norm=fox_k_norm[l], gdn_conv_w=gdn_conv_w[l],
                   gdn_norm=gdn_norm[l], w_branch_fox=w_branch_fox[l].astype(BF16),
                   w_branch_gdn=w_branch_gdn[l].astype(BF16), w_out=w_out[l].astype(BF16),
                   ffn_w_gate=ffn_w_gate[l].astype(BF16), ffn_w_up=ffn_w_up[l].astype(BF16),
                   ffn_conv_w=ffn_conv_w[l], ffn_w_down=ffn_w_down[l].astype(BF16))

        mod = _ada(c_all, w_ada[l], b_ada[l])
        mods_p = [mod[:nb, i * d:(i + 1) * d].reshape(nb, 1, d) for i in range(6)]
        mods_s = [mod[mc:, i * d:(i + 1) * d].reshape(1, bs, d) for i in range(6)]

        def fox_p(qn, kn, vb, small):
            f_hm = _cumf(small, nb, t)
            fpad = jnp.pad(f_hm[:, :nh].reshape(nb, nkv, g, t), ((0, 0), (0, 0), (0, SUBLANE - g), (0, 0)))
            return _fox_prompt(qn, kn, vb, fpad, nb, t, nkv, g)

        xp, st = _layer(xp, mods_p, wts, lay, prompt=True, nb=nb, tseq=t, fox_fn=fox_p, gdn_s0=None,
                        gdn_hist=None, ffn_hist=None)
        proj3 = st["proj"].reshape(nb, t, -1)
        outs_p[0].append(st["k"].reshape(nb, t, nkv, LANE))
        outs_p[1].append(st["v"].reshape(nb, t, nkv, LANE))
        outs_p[2].append(st["small"].reshape(nb, t, LANE)[:, :, :nh])
        outs_p[3].append(st["s"])
        outs_p[4].append(proj3[:, t - (gconv - 1):, cols["gqkv"]:cols["gqkv"] + 3 * kw])
        tail = st["tail"].reshape(nb, -1, SUBLANE, dff)
        outs_p[5].append(tail[:, -1, SUBLANE - (fconv - 1):, :])

        n_pool = cache_k.shape[1]
        kc = cache_k.reshape(depth * n_pool, page, nkv, LANE)
        vc = cache_v.reshape(depth * n_pool, page, nkv, LANE)
        lfc = cache_logf.reshape(depth * n_pool, page, nh)
        page_ids = page_table + l * n_pool

        def fox_s(qn, kn, vb, small):
            bm = lambda a, n: a.reshape(ts, bs, n, LANE).transpose(1, 0, 2, 3).reshape(bs, ts * n, LANE)
            nn = -(-ts * nkv // BF16_ROWS) * BF16_ROWS
            padr = lambda a, n: jnp.pad(a, ((0, 0), (0, n - a.shape[1]), (0, 0)))
            lfn = padr(small.reshape(ts, bs, LANE).transpose(1, 0, 2), BF16_ROWS)
            o = _fox_sample(page_ids, bm(qn, nh), padr(bm(kn, nkv), nn), padr(bm(vb, nkv), nn), lfn, kc, vc, lfc,
                            nkv, g, ts)
            return o.reshape(bs, ts, nh, LANE).transpose(1, 0, 2, 3).reshape(ts * bs, nh * LANE)

        ghist = state_gdn_conv[l].transpose(1, 0, 2).reshape((gconv - 1) * bs, 3 * kw)
        fhist = state_ffn_conv[l].transpose(1, 0, 2).reshape((fconv - 1) * bs, dff)
        xs, st = _layer(xs, mods_s, wts, lay, prompt=False, nb=bs, tseq=ts, fox_fn=fox_s, gdn_s0=state_gdn[l],
                        gdn_hist=ghist, ffn_hist=fhist)
        bm3 = lambda a: a.reshape(ts, bs, -1).transpose(1, 0, 2)
        outs_s[0].append(bm3(st["k"]).reshape(bs, ts, nkv, LANE))
        outs_s[1].append(bm3(st["v"]).reshape(bs, ts, nkv, LANE))
        outs_s[2].append(bm3(st["small"])[:, :, :nh])
        outs_s[3].append(st["s"])
        graw = bm3(st["proj"][:, cols["gqkv"]:cols["gqkv"] + 3 * kw])
        outs_s[4].append(jnp.concatenate([state_gdn_conv[l], graw], axis=1)[:, -(gconv - 1):])
        fraw = bm3(st["tail"].reshape(ts * bs, dff))
        outs_s[5].append(jnp.concatenate([state_ffn_conv[l], fraw], axis=1)[:, -(fconv - 1):])

    y_prompt = xp.reshape(nb, t, d)
    y_sample = xs.reshape(ts, bs, d).transpose(1, 0, 2)
    return (y_prompt, y_sample, *(jnp.stack(a) for a in outs_p), *(jnp.stack(a) for a in outs_s))
```

```python
import functools

import jax
import jax.numpy as jnp
from jax import lax
from jax.experimental import pallas as pl
from jax.experimental.pallas import tpu as pltpu

F32 = jnp.float32
BF16 = jnp.bfloat16
RMS_EPS = 1e-6
L2_EPS = 1e-6
NEG_INF = -1e30
LANE = 128
SUBLANE = 8
BF16_ROWS = 16
VMEM_LIMIT_BYTES = 56 * 2 ** 20

NN = (((1,), (0,)), ((), ()))
NT = (((1,), (1,)), ((), ()))
TN = (((0,), (0,)), ((), ()))


def _pick(n, pref, align):
    t = min(pref, n) // align * align
    while t >= align:
        if n % t == 0:
            return t
        t -= align
    return n


def _params(*sem):
    return pltpu.CompilerParams(dimension_semantics=sem, vmem_limit_bytes=VMEM_LIMIT_BYTES)


def _dot(a, b, dims=NN):
    return lax.dot_general(a.astype(BF16), b.astype(BF16), dims, preferred_element_type=F32)


def _dot_sel(sel, x, sel_first=True, dims=NN):
    h = x.astype(BF16)
    r = x - h.astype(F32)
    m = r.astype(BF16)
    l = (r - m.astype(F32)).astype(BF16)
    if sel_first:
        d = lambda p: lax.dot_general(sel, p, dims, preferred_element_type=F32)
    else:
        d = lambda p: lax.dot_general(p, sel, dims, preferred_element_type=F32)
    return d(h) + (d(m) + d(l))


def _silu(x):
    return x * jax.nn.sigmoid(x)


def _softplus(x):
    return jnp.maximum(x, 0.0) + jnp.log1p(jnp.exp(-jnp.abs(x)))


def _rows(a, m):
    if m.shape[0] == 1 or m.shape[0] == a.shape[0]:
        return m
    reps = a.shape[0] // m.shape[0]
    return jnp.concatenate([m] * reps, axis=0)


def _ada_kernel(c_ref, w_ref, b_ref, o_ref):
    a = _silu(c_ref[...])
    o_ref[...] = _dot(a, w_ref[...]) + b_ref[...]


def _ada(c_all, w_ada, b_ada):
    mc, d = c_all.shape
    n = w_ada.shape[1]
    tn = _pick(n, 512, LANE)
    return pl.pallas_call(
        _ada_kernel,
        grid=(n // tn,),
        in_specs=[pl.BlockSpec((mc, d), lambda j: (0, 0)),
                  pl.BlockSpec((d, tn), lambda j: (0, j)),
                  pl.BlockSpec((1, tn), lambda j: (0, j))],
        out_specs=pl.BlockSpec((mc, tn), lambda j: (0, j)),
        out_shape=jax.ShapeDtypeStruct((mc, n), F32),
        compiler_params=_params("parallel"),
        name="ada",
    )(c_all, w_ada, b_ada.reshape(1, n))


def _normmod_kernel(x_ref, g_ref, sc_ref, sh_ref, o_ref):
    x = x_ref[0]
    y = x * lax.rsqrt(jnp.mean(x * x, axis=-1, keepdims=True) + RMS_EPS) * g_ref[...]
    o_ref[0] = (y * (1.0 + _rows(y, sc_ref[0])) + _rows(y, sh_ref[0])).astype(o_ref.dtype)


def _normmod(x3, gain, sc3, sh3, tiles_per_mod):
    nt, tr, d = x3.shape
    mr = sc3.shape[1]
    mod_spec = pl.BlockSpec((1, mr, d), lambda i: (i // tiles_per_mod, 0, 0))
    return pl.pallas_call(
        _normmod_kernel,
        grid=(nt,),
        in_specs=[pl.BlockSpec((1, tr, d), lambda i: (i, 0, 0)),
                  pl.BlockSpec((1, d), lambda i: (0, 0)),
                  mod_spec, mod_spec],
        out_specs=pl.BlockSpec((1, tr, d), lambda i: (i, 0, 0)),
        out_shape=jax.ShapeDtypeStruct((nt, tr, d), BF16),
        compiler_params=_params("parallel"),
        name="normmod",
    )(x3, gain.reshape(1, d), sc3, sh3)


def _mm_kernel(x_ref, w_ref, o_ref):
    o_ref[...] = jnp.dot(x_ref[...], w_ref[...], preferred_element_type=F32).astype(o_ref.dtype)


def _matmul(x, w, tm, tn, out_dtype=F32):
    m, k = x.shape
    n = w.shape[1]
    return pl.pallas_call(
        _mm_kernel,
        grid=(m // tm, n // tn),
        in_specs=[pl.BlockSpec((tm, k), lambda i, j: (i, 0)),
                  pl.BlockSpec((k, tn), lambda i, j: (0, j))],
        out_specs=pl.BlockSpec((tm, tn), lambda i, j: (i, j)),
        out_shape=jax.ShapeDtypeStruct((m, n), out_dtype),
        compiler_params=_params("parallel", "arbitrary"),
        name="inproj",
    )(x, w)


def _small_kernel(x_ref, w_ref, p_ref, o_ref, *, nh):
    acc = jnp.dot(x_ref[...], w_ref[...], preferred_element_type=F32)
    xb = acc + p_ref[0:1, :]
    lane = lax.broadcasted_iota(jnp.int32, acc.shape, 1)
    sp = _softplus(xb)
    logf = jnp.minimum(xb, 0.0) - jnp.log1p(jnp.exp(-jnp.abs(xb)))
    g = -jnp.exp(p_ref[1:2, :]) * sp
    beta = jax.nn.sigmoid(acc)
    o_ref[...] = jnp.where(lane < nh, logf, jnp.where(lane < 2 * nh, g, jnp.where(lane < 3 * nh, beta, 0.0)))


def _small_heads(h, w_small, p_small, nh, tm):
    m, k = h.shape
    return pl.pallas_call(
        functools.partial(_small_kernel, nh=nh),
        grid=(m // tm,),
        in_specs=[pl.BlockSpec((tm, k), lambda i: (i, 0)),
                  pl.BlockSpec((k, LANE), lambda i: (0, 0)),
                  pl.BlockSpec((SUBLANE, LANE), lambda i: (0, 0))],
        out_specs=pl.BlockSpec((tm, LANE), lambda i: (i, 0)),
        out_shape=jax.ShapeDtypeStruct((m, LANE), F32),
        compiler_params=_params("parallel"),
        name="small_heads",
    )(h, w_small, p_small)


def _qknorm_kernel(x_ref, qg_ref, kg_ref, qn_ref, kn_ref, vb_ref, vt_ref, kf_ref, vf_ref, *, nq, nkv):
    def norm(x, g):
        return x * lax.rsqrt(jnp.mean(x * x, axis=-1, keepdims=True) + RMS_EPS) * g

    for h in range(nq):
        sl = slice(h * LANE, (h + 1) * LANE)
        qn_ref[:, sl] = norm(x_ref[:, sl], qg_ref[...]).astype(BF16)
    for h in range(nkv):
        sl = slice(h * LANE, (h + 1) * LANE)
        kx = norm(x_ref[:, (nq + h) * LANE:(nq + h + 1) * LANE], kg_ref[...])
        kf_ref[:, sl] = kx
        kn_ref[:, sl] = kx.astype(BF16)
        vx = x_ref[:, (nq + nkv + h) * LANE:(nq + nkv + h + 1) * LANE]
        vf_ref[:, sl] = vx
        vb_ref[:, sl] = vx.astype(BF16)
        vt_ref[sl, :] = vx.T.astype(BF16)


def _qknorm(proj, q_gain, k_gain, nq, nkv, tm):
    m = proj.shape[0]
    wq, wk = nq * LANE, nkv * LANE
    row = lambda w: pl.BlockSpec((tm, w), lambda i: (i, 0))
    gain = pl.BlockSpec((1, LANE), lambda i: (0, 0))
    return pl.pallas_call(
        functools.partial(_qknorm_kernel, nq=nq, nkv=nkv),
        grid=(m // tm,),
        in_specs=[row(wq + 2 * wk), gain, gain],
        out_specs=[row(wq), row(wk), row(wk), pl.BlockSpec((wk, tm), lambda i: (0, i)), row(wk), row(wk)],
        out_shape=[jax.ShapeDtypeStruct((m, wq), BF16), jax.ShapeDtypeStruct((m, wk), BF16),
                   jax.ShapeDtypeStruct((m, wk), BF16), jax.ShapeDtypeStruct((wk, m), BF16),
                   jax.ShapeDtypeStruct((m, wk), F32), jax.ShapeDtypeStruct((m, wk), F32)],
        compiler_params=_params("parallel"),
        name="qknorm",
    )(proj, q_gain.reshape(1, LANE), k_gain.reshape(1, LANE))


def _cumf_kernel(x_ref, o_ref, *, blk):
    t = x_ref.shape[0]
    r = lax.broadcasted_iota(jnp.int32, (blk, blk), 0)
    c = lax.broadcasted_iota(jnp.int32, (blk, blk), 1)
    tri = (r >= c).astype(BF16)
    carry = jnp.zeros((1, LANE), F32)
    for i in range(t // blk):
        cum = _dot_sel(tri, x_ref[i * blk:(i + 1) * blk, :]) + carry
        carry = cum[blk - 1:blk, :]
        o_ref[i * blk:(i + 1) * blk, :] = cum


def _cumf(small, nb, t):
    blk = _pick(t, 256, LANE)
    return pl.pallas_call(
        functools.partial(_cumf_kernel, blk=blk),
        grid=(nb,),
        in_specs=[pl.BlockSpec((t, LANE), lambda b: (b, 0))],
        out_specs=pl.BlockSpec((t, LANE), lambda b: (b, 0)),
        out_shape=jax.ShapeDtypeStruct((nb * t, LANE), F32),
        compiler_params=_params("parallel"),
        name="cumf",
    )(small)


def _fox_prompt_kernel(q_ref, k_ref, vt_ref, f_ref, o_ref, m_sc, l_sc, acc_sc, *, g, tq, scale):
    qi = pl.program_id(2)
    m_sc[...] = jnp.full(m_sc.shape, NEG_INF, F32)
    l_sc[...] = jnp.zeros(l_sc.shape, F32)
    acc_sc[...] = jnp.zeros(acc_sc.shape, F32)
    qs = jnp.concatenate([q_ref[:, i * LANE:(i + 1) * LANE] for i in range(g)], axis=0)

    def block(ki, masked):
        ks = pl.ds(pl.multiple_of(ki * tq, tq), tq)
        st = lax.dot_general(k_ref[ks, :], qs, NT, preferred_element_type=F32) * scale
        st = st - jnp.concatenate([jnp.broadcast_to(f_ref[0, 0, ks, i:i + 1], (tq, tq)) for i in range(g)], axis=1)
        if masked:
            keep = lax.broadcasted_iota(jnp.int32, (tq, tq), 0) <= lax.broadcasted_iota(jnp.int32, (tq, tq), 1)
            st = jnp.where(jnp.concatenate([keep] * g, axis=1), st, NEG_INF)
        m_prev = m_sc[...]
        m_new = jnp.maximum(m_prev, jnp.max(st, axis=0, keepdims=True))
        alpha = jnp.exp(m_prev - m_new)
        p = jnp.exp(st - m_new)
        l_sc[...] = alpha * l_sc[...] + jnp.sum(p, axis=0, keepdims=True)
        acc_sc[...] = alpha * acc_sc[...] + jnp.dot(vt_ref[:, ks], p.astype(BF16), preferred_element_type=F32)
        m_sc[...] = m_new

    def body(ki, carry):
        block(ki, False)
        return carry

    lax.fori_loop(0, qi, body, 0)
    block(qi, True)
    ot = acc_sc[...] / l_sc[...]
    for i in range(g):
        o_ref[:, i * LANE:(i + 1) * LANE] = ot[:, i * tq:(i + 1) * tq].T.astype(o_ref.dtype)


def _fox_prompt(qn, kn, vt, fcol, nb, t, nkv, g):
    tq = _pick(t, 512, LANE)
    nq = t // tq
    n = g * tq
    return pl.pallas_call(
        functools.partial(_fox_prompt_kernel, g=g, tq=tq, scale=LANE ** -0.5),
        grid=(nb, nkv, nq),
        in_specs=[pl.BlockSpec((tq, g * LANE), lambda b, h, qi: (b * nq + qi, h)),
                  pl.BlockSpec((t, LANE), lambda b, h, qi: (b, h)),
                  pl.BlockSpec((LANE, t), lambda b, h, qi: (h, b)),
                  pl.BlockSpec((1, 1, t, LANE), lambda b, h, qi: (b, h, 0, 0))],
        out_specs=pl.BlockSpec((tq, g * LANE), lambda b, h, qi: (b * nq + qi, h)),
        out_shape=jax.ShapeDtypeStruct((nb * t, nkv * g * LANE), BF16),
        scratch_shapes=[pltpu.VMEM((1, n), F32), pltpu.VMEM((1, n), F32), pltpu.VMEM((LANE, n), F32)],
        compiler_params=_params("parallel", "parallel", "arbitrary"),
        name="fox_prompt",
    )(qn, kn, vt, fcol)


def _fox_sample_kernel(pt_ref, q_ref, kn_ref, vn_ref, lfn_ref, *refs, npg, nkv, g, ts, scale):
    k_refs, v_refs, lf_refs = refs[:npg], refs[npg:2 * npg], refs[2 * npg:3 * npg]
    o_ref, m_sc, l_sc, acc_sc, f_sc = refs[3 * npg:]
    p = pl.program_id(1)
    page = k_refs[0].shape[1]
    nh = nkv * g
    rows = ts * nh
    cols = page * nkv

    @pl.when(p == 0)
    def _():
        m_sc[...] = jnp.full(m_sc.shape, NEG_INF, F32)
        l_sc[...] = jnp.zeros(l_sc.shape, F32)
        acc_sc[...] = jnp.zeros(acc_sc.shape, F32)
        f_sc[...] = jnp.zeros(f_sc.shape, F32)

    q = q_ref[0]

    def update(s, pv):
        m_prev = m_sc[...]
        m_new = jnp.maximum(m_prev, jnp.max(s, axis=-1, keepdims=True))
        alpha = jnp.exp(m_prev - m_new)
        pr = jnp.exp(s - m_new)
        l_sc[...] = alpha * l_sc[...] + jnp.sum(pr, axis=-1, keepdims=True)
        acc_sc[...] = alpha * acc_sc[...] + pv(pr.astype(BF16))
        m_sc[...] = m_new

    def own_kv(width):
        rq = lax.broadcasted_iota(jnp.int32, (rows, width), 0)
        cq = lax.broadcasted_iota(jnp.int32, (rows, width), 1)
        return rq, cq, (cq % nkv) == ((rq % nh) // g)

    ue = (lax.broadcasted_iota(jnp.int32, (page, cols), 0)
          <= lax.broadcasted_iota(jnp.int32, (page, cols), 1) // nkv).astype(BF16)
    _, _, own = own_kv(cols)
    carry = f_sc[...]
    parts = []
    for i in range(npg):
        fexp = _dot_sel(ue, lf_refs[i][0], sel_first=False, dims=TN) + carry[:, 0:1]
        carry = jnp.broadcast_to(fexp[:, cols - 1:cols], carry.shape)
        kf = k_refs[i][0].reshape(cols, LANE).astype(BF16)
        s = lax.dot_general(q, kf, NT, preferred_element_type=F32) * scale - jnp.concatenate([fexp] * ts, axis=0)
        parts.append(jnp.where(own, s, NEG_INF))
    f_sc[...] = carry

    def pv_pages(pr):
        out = None
        for i in range(npg):
            t_ = jnp.dot(pr[:, i * cols:(i + 1) * cols], v_refs[i][0].reshape(cols, LANE).astype(BF16),
                         preferred_element_type=F32)
            out = t_ if out is None else out + t_
        return out

    update(jnp.concatenate(parts, axis=1), pv_pages)

    @pl.when(p == pl.num_programs(1) - 1)
    def _():
        nn = kn_ref.shape[1]
        nl = lfn_ref.shape[1]
        uen = (lax.broadcasted_iota(jnp.int32, (nl, nn), 0)
               <= lax.broadcasted_iota(jnp.int32, (nl, nn), 1) // nkv).astype(BF16)
        fnew = _dot_sel(uen, lfn_ref[0], sel_first=False, dims=TN)[:nh] + f_sc[...][:, 0:1]
        rq, cq, own_n = own_kv(nn)
        keep = own_n & ((cq // nkv) <= (rq // nh)) & (cq < ts * nkv)
        s = lax.dot_general(q, kn_ref[0], NT, preferred_element_type=F32) * scale - jnp.concatenate([fnew] * ts, axis=0)
        update(jnp.where(keep, s, NEG_INF), lambda pr: jnp.dot(pr, vn_ref[0], preferred_element_type=F32))
        o_ref[0] = (acc_sc[...] / l_sc[...]).astype(o_ref.dtype)


def _fox_sample(page_ids, qb, knew, vnew, lfnew, kc, vc, lfc, nkv, g, ts):
    bs, rows, _ = qb.shape
    npages = page_ids.shape[1]
    npg = _pick(npages, 16, 1)
    page = kc.shape[1]
    nh = lfc.shape[2]
    nn = knew.shape[1]
    nl = lfnew.shape[1]

    def kv_page(i):
        return pl.BlockSpec((1, page, nkv, LANE), lambda b, p, pt, i=i: (pt[b * npages + p * npg + i], 0, 0, 0))

    def lf_page(i):
        return pl.BlockSpec((1, page, nh), lambda b, p, pt, i=i: (pt[b * npages + p * npg + i], 0, 0))

    per_b = lambda r: pl.BlockSpec((1, r, LANE), lambda b, p, pt: (b, 0, 0))
    in_specs = [per_b(rows), per_b(nn), per_b(nn), per_b(nl)]
    in_specs += [kv_page(i) for i in range(npg)] + [kv_page(i) for i in range(npg)] + [lf_page(i) for i in range(npg)]
    grid_spec = pltpu.PrefetchScalarGridSpec(
        num_scalar_prefetch=1,
        grid=(bs, npages // npg),
        in_specs=in_specs,
        out_specs=per_b(rows),
        scratch_shapes=[pltpu.VMEM((rows, 1), F32), pltpu.VMEM((rows, 1), F32), pltpu.VMEM((rows, LANE), F32),
                        pltpu.VMEM((nh, LANE), F32)],
    )
    return pl.pallas_call(
        functools.partial(_fox_sample_kernel, npg=npg, nkv=nkv, g=g, ts=ts, scale=LANE ** -0.5),
        grid_spec=grid_spec,
        out_shape=jax.ShapeDtypeStruct((bs, rows, LANE), BF16),
        compiler_params=_params("parallel", "arbitrary"),
        name="fox_sample",
    )(page_ids.reshape(-1), qb, knew, vnew, lfnew, *([kc] * npg), *([vc] * npg), *([lfc] * npg))


def _gdn_prep_kernel(halo_ref, x_ref, w_ref, o_ref, *, stride, taps, tiles_per_seq, zero_start, n_norm, n_q, scale):
    i, j = pl.program_id(0), pl.program_id(1)
    halo = halo_ref[...]
    if zero_start:
        halo = jnp.where(i % tiles_per_seq == 0, 0.0, halo)
    hh, tt = halo.shape[0], x_ref.shape[0]
    ext = jnp.concatenate([halo, x_ref[...]], axis=0)
    y = None
    for tap in range(taps):
        off = hh - (taps - 1 - tap) * stride
        term = ext[off:off + tt] * w_ref[tap:tap + 1, :]
        y = term if y is None else y + term
    y = _silu(y)
    is_norm = j < n_norm
    sc = jnp.where(j < n_q, scale, 1.0)
    for h in range(y.shape[1] // LANE):
        sl = slice(h * LANE, (h + 1) * LANE)
        yh = y[:, sl]
        nrm = yh * lax.rsqrt(jnp.sum(yh * yh, axis=-1, keepdims=True) + L2_EPS) * sc
        o_ref[:, sl] = jnp.where(is_norm, nrm, yh)


def _gdn_prep(x, col0, width, halo_arr, halo_is_x, conv_w, tt, tc, stride, tiles_per_seq, key_width, scale):
    m = x.shape[0]
    taps = conv_w.shape[0]
    cb = col0 // tc
    if halo_is_x:
        hh = SUBLANE
        halo_spec = pl.BlockSpec((hh, tc), lambda i, j: (jnp.maximum(i * (tt // hh) - 1, 0), cb + j))
        halo_in = x
    else:
        hh = halo_arr.shape[0]
        halo_spec = pl.BlockSpec((hh, tc), lambda i, j: (0, j))
        halo_in = halo_arr
    return pl.pallas_call(
        functools.partial(_gdn_prep_kernel, stride=stride, taps=taps, tiles_per_seq=tiles_per_seq,
                          zero_start=halo_is_x, n_norm=2 * key_width // tc, n_q=key_width // tc, scale=scale),
        grid=(m // tt, width // tc),
        in_specs=[halo_spec,
                  pl.BlockSpec((tt, tc), lambda i, j: (i, cb + j)),
                  pl.BlockSpec((taps, tc), lambda i, j: (0, j))],
        out_specs=pl.BlockSpec((tt, tc), lambda i, j: (i, j)),
        out_shape=jax.ShapeDtypeStruct((m, width), F32),
        compiler_params=_params("parallel", "parallel"),
        name="gdn_prep",
    )(halo_in, x, conv_w)


def _gdn_kernel(*refs, hb, c, use_s0):
    if use_s0:
        q_ref, k_ref, v_ref, z_ref, gb_ref, nw_ref, s0_ref, o_ref, so_ref, s_sc = refs
    else:
        q_ref, k_ref, v_ref, z_ref, gb_ref, nw_ref, o_ref, so_ref, s_sc = refs
    n = pl.program_id(2)

    @pl.when(n == 0)
    def _():
        s_sc[...] = s0_ref[0] if use_s0 else jnp.zeros(s_sc.shape, F32)

    heads = range(hb)
    split = lambda ref: jnp.stack([ref[:, h * LANE:(h + 1) * LANE] for h in heads])
    q, k, v = split(q_ref), split(k_ref), split(v_ref)
    gb = gb_ref[0]
    r = lax.broadcasted_iota(jnp.int32, (c, c), 0)
    cc = lax.broadcasted_iota(jnp.int32, (c, c), 1)
    incl, strict = r >= cc, r > cc
    gc_all = _dot_sel(incl.astype(BF16), gb)
    gc_t = gc_all.T
    gcol = jnp.stack([gc_all[:, h:h + 1] for h in heads])
    grow = jnp.stack([gc_t[h:h + 1, :] for h in heads])
    beta = jnp.stack([gb[:, hb + h:hb + h + 1] for h in heads])
    glast = gcol[:, c - 1:c, :]
    decay = jnp.exp(jnp.where(incl[None], gcol - grow, NEG_INF))
    kb = k * beta
    bdot = lambda a, b, eq: jnp.einsum(eq, a.astype(BF16), b.astype(BF16), preferred_element_type=F32)

    mm = jnp.where(strict[None], bdot(kb, k, 'hck,hsk->hcs') * decay, 0.0)
    eye = (r == cc).astype(F32)[None]
    base = min(c, SUBLANE)
    same = lambda s: ((r // s) == (cc // s))[None]
    mp = jnp.where(same(base), mm, 0.0)
    tinv = eye - mp
    span = 2
    while span < base:
        mp = bdot(mp, mp, 'hcs,hsk->hck')
        tinv = bdot(tinv, eye + mp, 'hcs,hsk->hck')
        span *= 2
    blk = base
    while blk < c:
        off = jnp.where(same(2 * blk) & ~same(blk), mm, 0.0)
        tinv = tinv - bdot(tinv, bdot(off, tinv, 'hcs,hsk->hck'), 'hcs,hsk->hck')
        blk *= 2
    egc = jnp.exp(gcol)
    u = bdot(tinv, v * beta, 'hcs,hsv->hcv')
    w = bdot(tinv, kb * egc, 'hcs,hsk->hck')
    a = bdot(q, k, 'hck,hsk->hcs') * decay
    qd = q * egc
    kd = k * jnp.exp(glast - gcol)
    s = s_sc[...]
    v_new = u - bdot(w, s, 'hck,hkv->hcv')
    o = bdot(qd, s, 'hck,hkv->hcv') + bdot(a, v_new, 'hcs,hsv->hcv')
    s_new = s * jnp.exp(glast) + bdot(kd, v_new, 'hck,hcv->hkv')
    s_sc[...] = s_new

    @pl.when(n == pl.num_programs(2) - 1)
    def _():
        so_ref[0] = s_new

    on = o * lax.rsqrt(jnp.mean(o * o, axis=-1, keepdims=True) + RMS_EPS) * nw_ref[...][None]
    for h in heads:
        sl = slice(h * LANE, (h + 1) * LANE)
        o_ref[:, sl] = (on[h] * _silu(z_ref[:, sl])).astype(o_ref.dtype)


def _gdn(qkv, z_arr, z_col0, gb, norm_w, s0, nb, tseq, c, hb, nheads):
    m = qkv.shape[0]
    n = tseq // c
    w = hb * LANE
    ng = nheads // hb
    kw = nheads * LANE
    row_col = lambda cb: pl.BlockSpec((c, w), lambda b, hg, i, cb=cb: (b * n + i, cb + hg))
    in_specs = [row_col(0), row_col(kw // w), row_col(2 * kw // w),
                pl.BlockSpec((c, w), lambda b, hg, i: (b * n + i, z_col0 // w + hg)),
                pl.BlockSpec((1, c, LANE), lambda b, hg, i: (hg, b * n + i, 0)),
                pl.BlockSpec((1, LANE), lambda b, hg, i: (0, 0))]
    args = [qkv, qkv, qkv, z_arr, gb, norm_w.reshape(1, LANE)]
    state_spec = pl.BlockSpec((1, hb, LANE, LANE), lambda b, hg, i: (b, hg, 0, 0))
    if s0 is not None:
        in_specs.append(state_spec)
        args.append(s0)
    return pl.pallas_call(
        functools.partial(_gdn_kernel, hb=hb, c=c, use_s0=s0 is not None),
        grid=(nb, ng, n),
        in_specs=in_specs,
        out_specs=[pl.BlockSpec((c, w), lambda b, hg, i: (b * n + i, hg)), state_spec],
        out_shape=[jax.ShapeDtypeStruct((m, kw), BF16), jax.ShapeDtypeStruct((nb, nheads, LANE, LANE), F32)],
        scratch_shapes=[pltpu.VMEM((hb, LANE, LANE), F32)],
        compiler_params=_params("parallel", "parallel", "arbitrary"),
        name="gdn",
    )(*args)


def _merge_kernel(of_ref, og_ref, wf_ref, wg_ref, gf_ref, gg_ref, o_ref):
    a = jnp.dot(of_ref[...], wf_ref[...].astype(BF16), preferred_element_type=F32)
    b = jnp.dot(og_ref[...], wg_ref[...].astype(BF16), preferred_element_type=F32)
    o_ref[...] = (jax.nn.sigmoid(gf_ref[...]) * a + jax.nn.sigmoid(gg_ref[...]) * b).astype(o_ref.dtype)


def _merge(o_fox, o_gdn, w_bf, w_bg, proj, gf_col0, gg_col0, tm, tn):
    m, kf = o_fox.shape
    kg = o_gdn.shape[1]
    n = w_bf.shape[1]
    return pl.pallas_call(
        _merge_kernel,
        grid=(m // tm, n // tn),
        in_specs=[pl.BlockSpec((tm, kf), lambda i, j: (i, 0)),
                  pl.BlockSpec((tm, kg), lambda i, j: (i, 0)),
                  pl.BlockSpec((kf, tn), lambda i, j: (0, j)),
                  pl.BlockSpec((kg, tn), lambda i, j: (0, j)),
                  pl.BlockSpec((tm, tn), lambda i, j: (i, gf_col0 // tn + j)),
                  pl.BlockSpec((tm, tn), lambda i, j: (i, gg_col0 // tn + j))],
        out_specs=pl.BlockSpec((tm, tn), lambda i, j: (i, j)),
        out_shape=jax.ShapeDtypeStruct((m, n), BF16),
        compiler_params=_params("parallel", "arbitrary"),
        name="merge",
    )(o_fox, o_gdn, w_bf, w_bg, proj, proj)


def _resid_kernel(a_ref, w_ref, x_ref, gt_ref, o_ref):
    acc = jnp.dot(a_ref[...], w_ref[...].astype(BF16), preferred_element_type=F32)
    o_ref[0] = x_ref[0] + _rows(acc, gt_ref[0]) * acc


def _resid_matmul(a, w, x3, gt3, tiles_per_mod, tn, name):
    nt, tm, n = x3.shape
    k = a.shape[1]
    mr = gt3.shape[1]
    return pl.pallas_call(
        _resid_kernel,
        grid=(nt, n // tn),
        in_specs=[pl.BlockSpec((tm, k), lambda i, j: (i, 0)),
                  pl.BlockSpec((k, tn), lambda i, j: (0, j)),
                  pl.BlockSpec((1, tm, tn), lambda i, j: (i, 0, j)),
                  pl.BlockSpec((1, mr, tn), lambda i, j: (i // tiles_per_mod, 0, j))],
        out_specs=pl.BlockSpec((1, tm, tn), lambda i, j: (i, 0, j)),
        out_shape=jax.ShapeDtypeStruct((nt, tm, n), F32),
        compiler_params=_params("parallel", "arbitrary"),
        name=name,
    )(a, w, x3, gt3)


def _ffn_up_kernel(*refs, stride, taps, tiles_per_seq, halo_from_h):
    if halo_from_h:
        hh_ref, h_ref, wg_ref, wu_ref, cw_ref, act_ref, tail_ref, hcat = refs
    else:
        hist_ref, h_ref, wg_ref, wu_ref, cw_ref, act_ref, tail_ref = refs
    i, j = pl.program_id(0), pl.program_id(1)
    tm = h_ref.shape[0]
    if halo_from_h:
        hh = hh_ref.shape[0]

        @pl.when(j == 0)
        def _():
            hcat[0:hh, :] = jnp.where(i % tiles_per_seq == 0, jnp.zeros_like(hh_ref[...]), hh_ref[...])
            hcat[hh:hh + tm, :] = h_ref[...]

        ext = jnp.dot(hcat[...], wg_ref[...].astype(BF16), preferred_element_type=F32)
    else:
        hh = hist_ref.shape[0]
        ext = jnp.concatenate(
            [hist_ref[...], jnp.dot(h_ref[...], wg_ref[...].astype(BF16), preferred_element_type=F32)], axis=0)
    up = jnp.dot(h_ref[...], wu_ref[...].astype(BF16), preferred_element_type=F32)
    y = None
    for tap in range(taps):
        off = hh - (taps - 1 - tap) * stride
        term = ext[off:off + tm] * cw_ref[tap:tap + 1, :]
        y = term if y is None else y + term
    act_ref[...] = (_silu(y) * up).astype(act_ref.dtype)
    tail_ref[0] = ext[hh + tm - tail_ref.shape[1]:hh + tm]


def _ffn_up(h, hist, w_gate, w_up, conv_w, tm, tn, stride, tiles_per_seq, tail_rows):
    m, k = h.shape
    n = w_gate.shape[1]
    taps = conv_w.shape[0]
    halo_from_h = hist is None
    if halo_from_h:
        hh = BF16_ROWS
        first = pl.BlockSpec((hh, k), lambda i, j: (jnp.maximum(i * (tm // hh) - 1, 0), 0))
        first_arg = h
        scratch = [pltpu.VMEM((hh + tm, k), BF16)]
    else:
        hh = hist.shape[0]
        first = pl.BlockSpec((hh, tn), lambda i, j: (0, j))
        first_arg = hist
        scratch = []
    return pl.pallas_call(
        functools.partial(_ffn_up_kernel, stride=stride, taps=taps, tiles_per_seq=tiles_per_seq,
                          halo_from_h=halo_from_h),
        grid=(m // tm, n // tn),
        in_specs=[first,
                  pl.BlockSpec((tm, k), lambda i, j: (i, 0)),
                  pl.BlockSpec((k, tn), lambda i, j: (0, j)),
                  pl.BlockSpec((k, tn), lambda i, j: (0, j)),
                  pl.BlockSpec((taps, tn), lambda i, j: (0, j))],
        out_specs=[pl.BlockSpec((tm, tn), lambda i, j: (i, j)),
                   pl.BlockSpec((1, tail_rows, tn), lambda i, j: (i, 0, j))],
        out_shape=[jax.ShapeDtypeStruct((m, n), BF16), jax.ShapeDtypeStruct((m // tm, tail_rows, n), F32)],
        scratch_shapes=scratch,
        compiler_params=_params("parallel", "arbitrary"),
        name="ffn_up",
    )(first_arg, h, w_gate, w_up, conv_w)


def _layer(x2, mods, wts, lay, *, prompt, nb, tseq, fox_fn, gdn_s0, gdn_hist, ffn_hist):
    m, d = x2.shape
    sh_m, sc_m, gt_m, sh_f, sc_f, gt_f = mods
    nh, nkv, g, gh = lay["nh"], lay["nkv"], lay["g"], lay["gh"]
    kw = gh * LANE
    if prompt:
        tr = _pick(tseq, 256, SUBLANE)
        tm = _pick(tseq, 1024, LANE)
        stride = 1
    else:
        tr = nb
        tm = m
        stride = nb
    tiles_mod = (tseq // tr) if prompt else m // tr
    h = _normmod(x2.reshape(m // tr, tr, d), wts["norm_mix"], sc_m, sh_m, tiles_mod).reshape(m, d)
    proj = _matmul(h, wts["w_main"], tm, _pick(wts["w_main"].shape[1], 512, LANE))
    small = _small_heads(h, wts["w_small"], wts["p_small"], nh, tm)
    c0 = lay["cols"]
    qn, kn, vb, vt, k_f, v_f = _qknorm(proj, wts["fox_q_norm"], wts["fox_k_norm"], nh, nkv, _pick(m, 256, LANE))
    o_fox = fox_fn(qn, kn, vb, vt, small)
    cw = 3 * kw
    tc = _pick(kw, 512, LANE)
    if prompt:
        tt = _pick(tseq, 512, SUBLANE)
        qkv = _gdn_prep(proj, c0["gqkv"], cw, None, True, wts["gdn_conv_w"], tt, tc, 1, tseq // tt, kw, LANE ** -0.5)
        hb = lay["hb"]
        gbp = _group_gb(small, nh, gh, hb)
        o_gdn, s_fin = _gdn(qkv, proj, c0["gz"], gbp, wts["gdn_norm"], None, nb, tseq, lay["chunk"], hb, gh)
    else:
        qkv = _gdn_prep(proj, c0["gqkv"], cw, gdn_hist, False, wts["gdn_conv_w"], m, tc, nb, 1, kw, LANE ** -0.5)
        ts = m // nb
        cpad = lay["cpad"]
        to_bm = lambda a: jnp.pad(a.reshape(ts, nb, -1).transpose(1, 0, 2),
                                  ((0, 0), (0, cpad - ts), (0, 0))).reshape(nb * cpad, -1)
        hb = lay["hb"]
        gbs = _group_gb(to_bm(small), nh, gh, hb)
        zs = to_bm(proj[:, c0["gz"]:c0["gz"] + kw])
        o_bm, s_fin = _gdn(to_bm(qkv), zs, 0, gbs, wts["gdn_norm"], gdn_s0, nb, cpad, cpad, hb, gh)
        o_gdn = o_bm.reshape(nb, cpad, kw)[:, :ts].transpose(1, 0, 2).reshape(m, kw)
    tn = _pick(d, 512, LANE)
    merged = _merge(o_fox, o_gdn, wts["w_branch_fox"], wts["w_branch_gdn"], proj, c0["gate_fox"], c0["gate_gdn"],
                    tm, tn)
    tiles_mod_m = (tseq // tm) if prompt else 1
    x3 = _resid_matmul(merged, wts["w_out"], x2.reshape(m // tm, tm, d), gt_m, tiles_mod_m, tn, "out_proj")
    x2 = x3.reshape(m, d)
    h2 = _normmod(x2.reshape(m // tr, tr, d), wts["norm_ffn"], sc_f, sh_f, tiles_mod).reshape(m, d)
    dff = wts["ffn_w_gate"].shape[1]
    tnf = _pick(dff, 256, LANE)
    act, tail = _ffn_up(h2, ffn_hist, wts["ffn_w_gate"], wts["ffn_w_up"], wts["ffn_conv_w"], tm, tnf, stride,
                        tseq // tm if prompt else 1, SUBLANE if prompt else m)
    tm2 = _pick(tm, 512, LANE) if prompt else m
    y3 = _resid_matmul(act, wts["ffn_w_down"], x2.reshape(m // tm2, tm2, d), gt_f,
                       (tseq // tm2) if prompt else 1, _pick(d, 256, LANE), "ffn_down")
    return y3.reshape(m, d), dict(proj=proj, small=small, k=k_f, v=v_f, s=s_fin, tail=tail)


def _group_gb(small, nh, gh, hb):
    rows = small.shape[0]
    gg = small[:, nh:nh + gh].reshape(rows, gh // hb, hb)
    bb = small[:, nh + gh:nh + 2 * gh].reshape(rows, gh // hb, hb)
    gb = jnp.concatenate([gg, bb], axis=-1).transpose(1, 0, 2)
    return jnp.pad(gb, ((0, 0), (0, 0), (0, LANE - 2 * hb)))


def kernel(x_prompt, x_sample, cache_k, cache_v, cache_logf, state_gdn, state_gdn_conv, state_ffn_conv, page_table, c_prompt, c_sample, w_ada, b_ada, norm_mix, norm_ffn, w_in, fox_b_f, fox_q_norm, fox_k_norm, gdn_conv_w, gdn_A_log, gdn_dt_bias, gdn_norm, w_branch_fox, w_branch_gdn, w_out, ffn_w_gate, ffn_w_up, ffn_conv_w, ffn_w_down):
    nb, t, d = x_prompt.shape
    bs, ts, _ = x_sample.shape
    depth = w_in.shape[0]
    page, nkv = cache_k.shape[2], cache_k.shape[3]
    nh = cache_logf.shape[-1]
    g = nh // nkv
    gh = state_gdn.shape[2]
    kw = gh * LANE
    gconv = gdn_conv_w.shape[1]
    fconv = ffn_conv_w.shape[1]
    dff = ffn_w_gate.shape[-1]
    assert cache_k.shape[-1] == LANE and state_gdn.shape[-1] == LANE and state_gdn.shape[-2] == LANE
    assert gh == nh and ts >= gconv - 1 and ts >= fconv - 1

    sizes = (nh * LANE, nkv * LANE, nkv * LANE, nh, 3 * kw, kw, gh, gh, d, d)
    starts = [0]
    for s_ in sizes:
        starts.append(starts[-1] + s_)
    assert starts[-1] == w_in.shape[-1]
    big = (0, 1, 2, 4, 5, 8, 9)
    names = ("fq", "fk", "fv", "gqkv", "gz", "gate_fox", "gate_gdn")
    cols, off = {}, 0
    for nm, i in zip(names, big):
        cols[nm] = off
        off += sizes[i]
    hb = _pick(gh, 16, 1)
    lay = dict(nh=nh, nkv=nkv, g=g, gh=gh, cols=cols, hb=hb, chunk=_pick(t, 64, SUBLANE), cpad=SUBLANE)

    xs = x_sample.transpose(1, 0, 2).reshape(ts * bs, d)
    xp = x_prompt.reshape(nb * t, d)
    mc = -(-nb // SUBLANE) * SUBLANE
    c_all = jnp.concatenate([c_prompt, jnp.zeros((mc - nb, d), F32), c_sample], axis=0)

    outs_p = [[] for _ in range(6)]
    outs_s = [[] for _ in range(6)]
    for l in range(depth):
        wi = w_in[l]
        w_main = jnp.concatenate([wi[:, starts[i]:starts[i + 1]] for i in big], axis=1).astype(BF16)
        w_small = jnp.concatenate([wi[:, starts[3]:starts[4]], wi[:, starts[6]:starts[7]], wi[:, starts[7]:starts[8]]],
                                  axis=1)
        w_small = jnp.pad(w_small, ((0, 0), (0, LANE - 3 * nh))).astype(BF16)
        zpad = jnp.zeros((LANE - 2 * nh,), F32)
        p_small = jnp.stack([jnp.concatenate([fox_b_f[l], gdn_dt_bias[l], zpad]),
                             jnp.concatenate([jnp.zeros((nh,), F32), gdn_A_log[l], zpad])])
        p_small = jnp.pad(p_small, ((0, SUBLANE - 2), (0, 0)))
        wts = dict(norm_mix=norm_mix[l], norm_ffn=norm_ffn[l], w_main=w_main, w_small=w_small, p_small=p_small,
                   fox_q_norm=fox_q_norm[l], fox_k_norm=fox_k_norm[l], gdn_conv_w=gdn_conv_w[l],
                   gdn_norm=gdn_norm[l], w_branch_fox=w_branch_fox[l], w_branch_gdn=w_branch_gdn[l],
                   w_out=w_out[l], ffn_w_gate=ffn_w_gate[l], ffn_w_up=ffn_w_up[l],
                   ffn_conv_w=ffn_conv_w[l], ffn_w_down=ffn_w_down[l].astype(BF16))

        mod = _ada(c_all, w_ada[l], b_ada[l])
        mods_p = [mod[:nb, i * d:(i + 1) * d].reshape(nb, 1, d) for i in range(6)]
        mods_s = [mod[mc:, i * d:(i + 1) * d].reshape(1, bs, d) for i in range(6)]

        def fox_p(qn, kn, vb, vt, small):
            f_tm = _cumf(small, nb, t)
            fcol = f_tm[:, :nh].reshape(nb, t, nkv, g).transpose(0, 2, 1, 3)
            fcol = jnp.pad(fcol, ((0, 0), (0, 0), (0, 0), (0, LANE - g)))
            return _fox_prompt(qn, kn, vt, fcol, nb, t, nkv, g)

        xp, st = _layer(xp, mods_p, wts, lay, prompt=True, nb=nb, tseq=t, fox_fn=fox_p, gdn_s0=None,
                        gdn_hist=None, ffn_hist=None)
        proj3 = st["proj"].reshape(nb, t, -1)
        outs_p[0].append(st["k"].reshape(nb, t, nkv, LANE))
        outs_p[1].append(st["v"].reshape(nb, t, nkv, LANE))
        outs_p[2].append(st["small"].reshape(nb, t, LANE)[:, :, :nh])
        outs_p[3].append(st["s"])
        outs_p[4].append(proj3[:, t - (gconv - 1):, cols["gqkv"]:cols["gqkv"] + 3 * kw])
        tail = st["tail"].reshape(nb, -1, SUBLANE, dff)
        outs_p[5].append(tail[:, -1, SUBLANE - (fconv - 1):, :])

        n_pool = cache_k.shape[1]
        kc = cache_k.reshape(depth * n_pool, page, nkv, LANE)
        vc = cache_v.reshape(depth * n_pool, page, nkv, LANE)
        lfc = cache_logf.reshape(depth * n_pool, page, nh)
        page_ids = page_table + l * n_pool

        def fox_s(qn, kn, vb, vt, small):
            bm = lambda a, n: a.reshape(ts, bs, n, LANE).transpose(1, 0, 2, 3).reshape(bs, ts * n, LANE)
            nn = -(-ts * nkv // BF16_ROWS) * BF16_ROWS
            padr = lambda a, n: jnp.pad(a, ((0, 0), (0, n - a.shape[1]), (0, 0)))
            lfn = padr(small.reshape(ts, bs, LANE).transpose(1, 0, 2), BF16_ROWS)
            o = _fox_sample(page_ids, bm(qn, nh), padr(bm(kn, nkv), nn), padr(bm(vb, nkv), nn), lfn, kc, vc, lfc,
                            nkv, g, ts)
            return o.reshape(bs, ts, nh, LANE).transpose(1, 0, 2, 3).reshape(ts * bs, nh * LANE)

        ghist = state_gdn_conv[l].transpose(1, 0, 2).reshape((gconv - 1) * bs, 3 * kw)
        fhist = state_ffn_conv[l].transpose(1, 0, 2).reshape((fconv - 1) * bs, dff)
        xs, st = _layer(xs, mods_s, wts, lay, prompt=False, nb=bs, tseq=ts, fox_fn=fox_s, gdn_s0=state_gdn[l],
                        gdn_hist=ghist, ffn_hist=fhist)
        bm3 = lambda a: a.reshape(ts, bs, -1).transpose(1, 0, 2)
        outs_s[0].append(bm3(st["k"]).reshape(bs, ts, nkv, LANE))
        outs_s[1].append(bm3(st["v"]).reshape(bs, ts, nkv, LANE))
        outs_s[2].append(bm3(st["small"])[:, :, :nh])
        outs_s[3].append(st["s"])
        graw = bm3(st["proj"][:, cols["gqkv"]:cols["gqkv"] + 3 * kw])
        outs_s[4].append(jnp.concatenate([state_gdn_conv[l], graw], axis=1)[:, -(gconv - 1):])
        fraw = bm3(st["tail"].reshape(ts * bs, dff))
        outs_s[5].append(jnp.concatenate([state_ffn_conv[l], fraw], axis=1)[:, -(fconv - 1):])

    y_prompt = xp.reshape(nb, t, d)
    y_sample = xs.reshape(ts, bs, d).transpose(1, 0, 2)
    return (y_prompt, y_sample, *(jnp.stack(a) for a in outs_p), *(jnp.stack(a) for a in outs_s))
```

```python
import functools

import jax
import jax.numpy as jnp
from jax import lax
from jax.experimental import pallas as pl
from jax.experimental.pallas import tpu as pltpu

F32 = jnp.float32
BF16 = jnp.bfloat16
RMS_EPS = 1e-6
L2_EPS = 1e-6
NEG_INF = -1e30
LANE = 128
SUBLANE = 8
BF16_ROWS = 16
VMEM_LIMIT_BYTES = 56 * 2 ** 20

NN = (((1,), (0,)), ((), ()))
NT = (((1,), (1,)), ((), ()))
TN = (((0,), (0,)), ((), ()))


def _pick(n, pref, align):
    t = min(pref, n) // align * align
    while t >= align:
        if n % t == 0:
            return t
        t -= align
    return n


def _params(*sem):
    return pltpu.CompilerParams(dimension_semantics=sem, vmem_limit_bytes=VMEM_LIMIT_BYTES)


def _dot(a, b, dims=NN):
    return lax.dot_general(a.astype(BF16), b.astype(BF16), dims, preferred_element_type=F32)


def _dot_sel(sel, x, sel_first=True, dims=NN):
    h = x.astype(BF16)
    r = x - h.astype(F32)
    m = r.astype(BF16)
    l = (r - m.astype(F32)).astype(BF16)
    if sel_first:
        d = lambda p: lax.dot_general(sel, p, dims, preferred_element_type=F32)
    else:
        d = lambda p: lax.dot_general(p, sel, dims, preferred_element_type=F32)
    return d(h) + (d(m) + d(l))


def _silu(x):
    return x * jax.nn.sigmoid(x)


def _softplus(x):
    return jnp.maximum(x, 0.0) + jnp.log1p(jnp.exp(-jnp.abs(x)))


def _rows(a, m):
    if m.shape[0] == 1 or m.shape[0] == a.shape[0]:
        return m
    reps = a.shape[0] // m.shape[0]
    return jnp.concatenate([m] * reps, axis=0)


def _ada_kernel(c_ref, w_ref, b_ref, o_ref):
    a = _silu(c_ref[...])
    o_ref[...] = _dot(a, w_ref[...]) + b_ref[...]


def _ada(c_all, w_ada, b_ada):
    mc, d = c_all.shape
    n = w_ada.shape[1]
    tn = _pick(n, 512, LANE)
    return pl.pallas_call(
        _ada_kernel,
        grid=(n // tn,),
        in_specs=[pl.BlockSpec((mc, d), lambda j: (0, 0)),
                  pl.BlockSpec((d, tn), lambda j: (0, j)),
                  pl.BlockSpec((1, tn), lambda j: (0, j))],
        out_specs=pl.BlockSpec((mc, tn), lambda j: (0, j)),
        out_shape=jax.ShapeDtypeStruct((mc, n), F32),
        compiler_params=_params("parallel"),
        name="ada",
    )(c_all, w_ada, b_ada.reshape(1, n))


def _normmod_kernel(x_ref, g_ref, sc_ref, sh_ref, o_ref):
    x = x_ref[0]
    y = x * lax.rsqrt(jnp.mean(x * x, axis=-1, keepdims=True) + RMS_EPS) * g_ref[...]
    o_ref[0] = (y * (1.0 + _rows(y, sc_ref[0])) + _rows(y, sh_ref[0])).astype(o_ref.dtype)


def _normmod(x3, gain, sc3, sh3, tiles_per_mod):
    nt, tr, d = x3.shape
    mr = sc3.shape[1]
    mod_spec = pl.BlockSpec((1, mr, d), lambda i: (i // tiles_per_mod, 0, 0))
    return pl.pallas_call(
        _normmod_kernel,
        grid=(nt,),
        in_specs=[pl.BlockSpec((1, tr, d), lambda i: (i, 0, 0)),
                  pl.BlockSpec((1, d), lambda i: (0, 0)),
                  mod_spec, mod_spec],
        out_specs=pl.BlockSpec((1, tr, d), lambda i: (i, 0, 0)),
        out_shape=jax.ShapeDtypeStruct((nt, tr, d), BF16),
        compiler_params=_params("parallel"),
        name="normmod",
    )(x3, gain.reshape(1, d), sc3, sh3)


def _wprep_kernel(a_ref, b_ref, o_ref, *, delta):
    a = a_ref[...]
    if delta:
        a = jnp.concatenate([a, b_ref[...]], axis=1)[:, delta:delta + a.shape[1]]
    o_ref[...] = a.astype(BF16)


def _wprep(w2, row0, rows, col0, width):
    tc = _pick(width, 512, LANE)
    tr = _pick(rows, 1024, SUBLANE)
    delta, cb, rb = col0 % tc, col0 // tc, row0 // tr
    assert row0 % tr == 0 and delta < LANE
    nxt = (lambda i, j: (rb + i, (cb + j + 1) * (tc // LANE))) if delta else (lambda i, j: (rb + i, 0))
    return pl.pallas_call(
        functools.partial(_wprep_kernel, delta=delta),
        grid=(rows // tr, width // tc),
        in_specs=[pl.BlockSpec((tr, tc), lambda i, j: (rb + i, cb + j)), pl.BlockSpec((tr, LANE), nxt)],
        out_specs=pl.BlockSpec((tr, tc), lambda i, j: (i, j)),
        out_shape=jax.ShapeDtypeStruct((rows, width), BF16),
        compiler_params=_params("parallel", "parallel"),
        name="wprep",
    )(w2, w2)


def _mm_kernel(x_ref, w_ref, o_ref):
    o_ref[...] = jnp.dot(x_ref[...], w_ref[...], preferred_element_type=F32).astype(o_ref.dtype)


def _matmul(x, w, tm, tn, out_dtype=F32):
    m, k = x.shape
    n = w.shape[1]
    return pl.pallas_call(
        _mm_kernel,
        grid=(m // tm, n // tn),
        in_specs=[pl.BlockSpec((tm, k), lambda i, j: (i, 0)),
                  pl.BlockSpec((k, tn), lambda i, j: (0, j))],
        out_specs=pl.BlockSpec((tm, tn), lambda i, j: (i, j)),
        out_shape=jax.ShapeDtypeStruct((m, n), out_dtype),
        compiler_params=_params("parallel", "arbitrary"),
        name="inproj",
    )(x, w)


def _small_kernel(x_ref, w_ref, p_ref, o_ref, *, nh):
    acc = jnp.dot(x_ref[...], w_ref[...], preferred_element_type=F32)
    xb = acc + p_ref[0:1, :]
    lane = lax.broadcasted_iota(jnp.int32, acc.shape, 1)
    sp = _softplus(xb)
    logf = jnp.minimum(xb, 0.0) - jnp.log1p(jnp.exp(-jnp.abs(xb)))
    g = -jnp.exp(p_ref[1:2, :]) * sp
    beta = jax.nn.sigmoid(acc)
    o_ref[...] = jnp.where(lane < nh, logf, jnp.where(lane < 2 * nh, g, jnp.where(lane < 3 * nh, beta, 0.0)))


def _small_heads(h, w_small, p_small, nh, tm):
    m, k = h.shape
    return pl.pallas_call(
        functools.partial(_small_kernel, nh=nh),
        grid=(m // tm,),
        in_specs=[pl.BlockSpec((tm, k), lambda i: (i, 0)),
                  pl.BlockSpec((k, LANE), lambda i: (0, 0)),
                  pl.BlockSpec((SUBLANE, LANE), lambda i: (0, 0))],
        out_specs=pl.BlockSpec((tm, LANE), lambda i: (i, 0)),
        out_shape=jax.ShapeDtypeStruct((m, LANE), F32),
        compiler_params=_params("parallel"),
        name="small_heads",
    )(h, w_small, p_small)


def _qknorm_kernel(x_ref, qg_ref, kg_ref, qn_ref, kn_ref, vb_ref, vt_ref, kf_ref, vf_ref, *, nq, nkv):
    def norm(x, g):
        return x * lax.rsqrt(jnp.mean(x * x, axis=-1, keepdims=True) + RMS_EPS) * g

    for h in range(nq):
        sl = slice(h * LANE, (h + 1) * LANE)
        qn_ref[:, sl] = norm(x_ref[:, sl], qg_ref[...]).astype(BF16)
    for h in range(nkv):
        sl = slice(h * LANE, (h + 1) * LANE)
        kx = norm(x_ref[:, (nq + h) * LANE:(nq + h + 1) * LANE], kg_ref[...])
        kf_ref[:, sl] = kx
        kn_ref[:, sl] = kx.astype(BF16)
        vx = x_ref[:, (nq + nkv + h) * LANE:(nq + nkv + h + 1) * LANE]
        vf_ref[:, sl] = vx
        vb_ref[:, sl] = vx.astype(BF16)
        vt_ref[sl, :] = vx.T.astype(BF16)


def _qknorm(proj, q_gain, k_gain, nq, nkv, tm):
    m = proj.shape[0]
    wq, wk = nq * LANE, nkv * LANE
    row = lambda w: pl.BlockSpec((tm, w), lambda i: (i, 0))
    gain = pl.BlockSpec((1, LANE), lambda i: (0, 0))
    return pl.pallas_call(
        functools.partial(_qknorm_kernel, nq=nq, nkv=nkv),
        grid=(m // tm,),
        in_specs=[row(wq + 2 * wk), gain, gain],
        out_specs=[row(wq), row(wk), row(wk), pl.BlockSpec((wk, tm), lambda i: (0, i)), row(wk), row(wk)],
        out_shape=[jax.ShapeDtypeStruct((m, wq), BF16), jax.ShapeDtypeStruct((m, wk), BF16),
                   jax.ShapeDtypeStruct((m, wk), BF16), jax.ShapeDtypeStruct((wk, m), BF16),
                   jax.ShapeDtypeStruct((m, wk), F32), jax.ShapeDtypeStruct((m, wk), F32)],
        compiler_params=_params("parallel"),
        name="qknorm",
    )(proj, q_gain.reshape(1, LANE), k_gain.reshape(1, LANE))


def _cumf_kernel(x_ref, o_ref, *, blk):
    t = x_ref.shape[0]
    r = lax.broadcasted_iota(jnp.int32, (blk, blk), 0)
    c = lax.broadcasted_iota(jnp.int32, (blk, blk), 1)
    tri = (r >= c).astype(BF16)
    carry = jnp.zeros((1, LANE), F32)
    for i in range(t // blk):
        cum = _dot_sel(tri, x_ref[i * blk:(i + 1) * blk, :]) + carry
        carry = cum[blk - 1:blk, :]
        o_ref[i * blk:(i + 1) * blk, :] = cum


def _cumf(small, nb, t):
    blk = _pick(t, 256, LANE)
    return pl.pallas_call(
        functools.partial(_cumf_kernel, blk=blk),
        grid=(nb,),
        in_specs=[pl.BlockSpec((t, LANE), lambda b: (b, 0))],
        out_specs=pl.BlockSpec((t, LANE), lambda b: (b, 0)),
        out_shape=jax.ShapeDtypeStruct((nb * t, LANE), F32),
        compiler_params=_params("parallel"),
        name="cumf",
    )(small)


def _fox_prompt_kernel(q_ref, k_ref, vt_ref, f_ref, o_ref, m_sc, l_sc, acc_sc, *, g, tq, scale):
    qi = pl.program_id(2)
    m_sc[...] = jnp.full(m_sc.shape, NEG_INF, F32)
    l_sc[...] = jnp.zeros(l_sc.shape, F32)
    acc_sc[...] = jnp.zeros(acc_sc.shape, F32)
    qs = jnp.concatenate([q_ref[:, i * LANE:(i + 1) * LANE] for i in range(g)], axis=0)

    def block(ki, masked):
        ks = pl.ds(pl.multiple_of(ki * tq, tq), tq)
        st = lax.dot_general(k_ref[ks, :], qs, NT, preferred_element_type=F32) * scale
        st = st - jnp.concatenate([jnp.broadcast_to(f_ref[0, 0, ks, i:i + 1], (tq, tq)) for i in range(g)], axis=1)
        if masked:
            keep = lax.broadcasted_iota(jnp.int32, (tq, tq), 0) <= lax.broadcasted_iota(jnp.int32, (tq, tq), 1)
            st = jnp.where(jnp.concatenate([keep] * g, axis=1), st, NEG_INF)
        m_prev = m_sc[...]
        m_new = jnp.maximum(m_prev, jnp.max(st, axis=0, keepdims=True))
        alpha = jnp.exp(m_prev - m_new)
        p = jnp.exp(st - m_new)
        l_sc[...] = alpha * l_sc[...] + jnp.sum(p, axis=0, keepdims=True)
        acc_sc[...] = alpha * acc_sc[...] + jnp.dot(vt_ref[:, ks], p.astype(BF16), preferred_element_type=F32)
        m_sc[...] = m_new

    def body(ki, carry):
        block(ki, False)
        return carry

    lax.fori_loop(0, qi, body, 0)
    block(qi, True)
    ot = acc_sc[...] / l_sc[...]
    for i in range(g):
        o_ref[:, i * LANE:(i + 1) * LANE] = ot[:, i * tq:(i + 1) * tq].T.astype(o_ref.dtype)


def _fox_prompt(qn, kn, vt, fcol, nb, t, nkv, g):
    tq = _pick(t, 512, LANE)
    nq = t // tq
    n = g * tq
    return pl.pallas_call(
        functools.partial(_fox_prompt_kernel, g=g, tq=tq, scale=LANE ** -0.5),
        grid=(nb, nkv, nq),
        in_specs=[pl.BlockSpec((tq, g * LANE), lambda b, h, qi: (b * nq + qi, h)),
                  pl.BlockSpec((t, LANE), lambda b, h, qi: (b, h)),
                  pl.BlockSpec((LANE, t), lambda b, h, qi: (h, b)),
                  pl.BlockSpec((1, 1, t, LANE), lambda b, h, qi: (b, h, 0, 0))],
        out_specs=pl.BlockSpec((tq, g * LANE), lambda b, h, qi: (b * nq + qi, h)),
        out_shape=jax.ShapeDtypeStruct((nb * t, nkv * g * LANE), BF16),
        scratch_shapes=[pltpu.VMEM((1, n), F32), pltpu.VMEM((1, n), F32), pltpu.VMEM((LANE, n), F32)],
        compiler_params=_params("parallel", "parallel", "arbitrary"),
        name="fox_prompt",
    )(qn, kn, vt, fcol)


def _fox_sample_kernel(pt_ref, q_ref, kn_ref, vn_ref, lfn_ref, *refs, npg, nkv, g, ts, scale):
    k_refs, v_refs, lf_refs = refs[:npg], refs[npg:2 * npg], refs[2 * npg:3 * npg]
    o_ref, m_sc, l_sc, acc_sc, f_sc = refs[3 * npg:]
    p = pl.program_id(1)
    page = k_refs[0].shape[1]
    nh = nkv * g
    rows = ts * nh
    cols = page * nkv

    @pl.when(p == 0)
    def _():
        m_sc[...] = jnp.full(m_sc.shape, NEG_INF, F32)
        l_sc[...] = jnp.zeros(l_sc.shape, F32)
        acc_sc[...] = jnp.zeros(acc_sc.shape, F32)
        f_sc[...] = jnp.zeros(f_sc.shape, F32)

    q = q_ref[0]

    def update(s, pv):
        m_prev = m_sc[...]
        m_new = jnp.maximum(m_prev, jnp.max(s, axis=-1, keepdims=True))
        alpha = jnp.exp(m_prev - m_new)
        pr = jnp.exp(s - m_new)
        l_sc[...] = alpha * l_sc[...] + jnp.sum(pr, axis=-1, keepdims=True)
        acc_sc[...] = alpha * acc_sc[...] + pv(pr.astype(BF16))
        m_sc[...] = m_new

    def own_kv(width):
        rq = lax.broadcasted_iota(jnp.int32, (rows, width), 0)
        cq = lax.broadcasted_iota(jnp.int32, (rows, width), 1)
        return rq, cq, (cq % nkv) == ((rq % nh) // g)

    ue = (lax.broadcasted_iota(jnp.int32, (page, cols), 0)
          <= lax.broadcasted_iota(jnp.int32, (page, cols), 1) // nkv).astype(BF16)
    _, _, own = own_kv(cols)
    carry = f_sc[...]
    parts = []
    for i in range(npg):
        fexp = _dot_sel(ue, lf_refs[i][0], sel_first=False, dims=TN) + carry[:, 0:1]
        carry = jnp.broadcast_to(fexp[:, cols - 1:cols], carry.shape)
        kf = k_refs[i][0].reshape(cols, LANE).astype(BF16)
        s = lax.dot_general(q, kf, NT, preferred_element_type=F32) * scale - jnp.concatenate([fexp] * ts, axis=0)
        parts.append(jnp.where(own, s, NEG_INF))
    f_sc[...] = carry

    def pv_pages(pr):
        out = None
        for i in range(npg):
            t_ = jnp.dot(pr[:, i * cols:(i + 1) * cols], v_refs[i][0].reshape(cols, LANE).astype(BF16),
                         preferred_element_type=F32)
            out = t_ if out is None else out + t_
        return out

    update(jnp.concatenate(parts, axis=1), pv_pages)

    @pl.when(p == pl.num_programs(1) - 1)
    def _():
        nn = kn_ref.shape[1]
        nl = lfn_ref.shape[1]
        uen = (lax.broadcasted_iota(jnp.int32, (nl, nn), 0)
               <= lax.broadcasted_iota(jnp.int32, (nl, nn), 1) // nkv).astype(BF16)
        fnew = _dot_sel(uen, lfn_ref[0], sel_first=False, dims=TN)[:nh] + f_sc[...][:, 0:1]
        rq, cq, own_n = own_kv(nn)
        keep = own_n & ((cq // nkv) <= (rq // nh)) & (cq < ts * nkv)
        s = lax.dot_general(q, kn_ref[0], NT, preferred_element_type=F32) * scale - jnp.concatenate([fnew] * ts, axis=0)
        update(jnp.where(keep, s, NEG_INF), lambda pr: jnp.dot(pr, vn_ref[0], preferred_element_type=F32))
        o_ref[0] = (acc_sc[...] / l_sc[...]).astype(o_ref.dtype)


def _fox_sample(page_ids, qb, knew, vnew, lfnew, kc, vc, lfc, nkv, g, ts):
    bs, rows, _ = qb.shape
    npages = page_ids.shape[1]
    npg = _pick(npages, 16, 1)
    page = kc.shape[1]
    nh = lfc.shape[2]
    nn = knew.shape[1]
    nl = lfnew.shape[1]

    def kv_page(i):
        return pl.BlockSpec((1, page, nkv, LANE), lambda b, p, pt, i=i: (pt[b * npages + p * npg + i], 0, 0, 0))

    def lf_page(i):
        return pl.BlockSpec((1, page, nh), lambda b, p, pt, i=i: (pt[b * npages + p * npg + i], 0, 0))

    per_b = lambda r: pl.BlockSpec((1, r, LANE), lambda b, p, pt: (b, 0, 0))
    in_specs = [per_b(rows), per_b(nn), per_b(nn), per_b(nl)]
    in_specs += [kv_page(i) for i in range(npg)] + [kv_page(i) for i in range(npg)] + [lf_page(i) for i in range(npg)]
    grid_spec = pltpu.PrefetchScalarGridSpec(
        num_scalar_prefetch=1,
        grid=(bs, npages // npg),
        in_specs=in_specs,
        out_specs=per_b(rows),
        scratch_shapes=[pltpu.VMEM((rows, 1), F32), pltpu.VMEM((rows, 1), F32), pltpu.VMEM((rows, LANE), F32),
                        pltpu.VMEM((nh, LANE), F32)],
    )
    return pl.pallas_call(
        functools.partial(_fox_sample_kernel, npg=npg, nkv=nkv, g=g, ts=ts, scale=LANE ** -0.5),
        grid_spec=grid_spec,
        out_shape=jax.ShapeDtypeStruct((bs, rows, LANE), BF16),
        compiler_params=_params("parallel", "arbitrary"),
        name="fox_sample",
    )(page_ids.reshape(-1), qb, knew, vnew, lfnew, *([kc] * npg), *([vc] * npg), *([lfc] * npg))


def _gdn_kernel(*refs, hb, c, taps, use_state, scale):
    if use_state:
        (q_ref, k_ref, v_ref, wq_ref, wk_ref, wv_ref, z_ref, gb_ref, nw_ref, hq_ref, hk_ref, hv_ref, s0_ref,
         o_ref, so_ref, s_sc, tail_sc) = refs
    else:
        q_ref, k_ref, v_ref, wq_ref, wk_ref, wv_ref, z_ref, gb_ref, nw_ref, o_ref, so_ref, s_sc, tail_sc = refs
    n = pl.program_id(2)

    @pl.when(n == 0)
    def _():
        s_sc[...] = s0_ref[0] if use_state else jnp.zeros(s_sc.shape, F32)
        if use_state:
            for i, h_ref in enumerate((hq_ref, hk_ref, hv_ref)):
                tail_sc[i] = h_ref[...]
        else:
            tail_sc[...] = jnp.zeros(tail_sc.shape, F32)

    heads = range(hb)

    def conv_act(x_ref, w_ref, idx, l2_scale):
        x = x_ref[...]
        ext = jnp.concatenate([tail_sc[idx], x], axis=0)
        tail_sc[idx] = x[c - SUBLANE:c]
        y = None
        for tap in range(taps):
            off = SUBLANE - (taps - 1 - tap)
            term = ext[off:off + c] * w_ref[tap:tap + 1, :]
            y = term if y is None else y + term
        y = _silu(y)
        cols = [y[:, h * LANE:(h + 1) * LANE] for h in heads]
        if l2_scale is not None:
            cols = [yh * (lax.rsqrt(jnp.sum(yh * yh, axis=-1, keepdims=True) + L2_EPS) * l2_scale) for yh in cols]
        return jnp.stack(cols)

    q = conv_act(q_ref, wq_ref, 0, scale)
    k = conv_act(k_ref, wk_ref, 1, 1.0)
    v = conv_act(v_ref, wv_ref, 2, None)
    gb = gb_ref[0]
    r = lax.broadcasted_iota(jnp.int32, (c, c), 0)
    cc = lax.broadcasted_iota(jnp.int32, (c, c), 1)
    incl, strict = r >= cc, r > cc
    gc_all = _dot_sel(incl.astype(BF16), gb)
    gc_t = gc_all.T
    gcol = jnp.stack([gc_all[:, h:h + 1] for h in heads])
    grow = jnp.stack([gc_t[h:h + 1, :] for h in heads])
    beta = jnp.stack([gb[:, hb + h:hb + h + 1] for h in heads])
    glast = gcol[:, c - 1:c, :]
    decay = jnp.exp(jnp.where(incl[None], gcol - grow, NEG_INF))
    kb = k * beta
    bdot = lambda a, b, eq: jnp.einsum(eq, a.astype(BF16), b.astype(BF16), preferred_element_type=F32)

    mm = jnp.where(strict[None], bdot(kb, k, 'hck,hsk->hcs') * decay, 0.0)
    eye = (r == cc).astype(F32)[None]
    base = min(c, SUBLANE)
    same = lambda s: ((r // s) == (cc // s))[None]
    mp = jnp.where(same(base), mm, 0.0)
    tinv = eye - mp
    span = 2
    while span < base:
        mp = bdot(mp, mp, 'hcs,hsk->hck')
        tinv = bdot(tinv, eye + mp, 'hcs,hsk->hck')
        span *= 2
    blk = base
    while blk < c:
        off = jnp.where(same(2 * blk) & ~same(blk), mm, 0.0)
        tinv = tinv - bdot(tinv, bdot(off, tinv, 'hcs,hsk->hck'), 'hcs,hsk->hck')
        blk *= 2
    egc = jnp.exp(gcol)
    u = bdot(tinv, v * beta, 'hcs,hsv->hcv')
    w = bdot(tinv, kb * egc, 'hcs,hsk->hck')
    a = bdot(q, k, 'hck,hsk->hcs') * decay
    qd = q * egc
    kd = k * jnp.exp(glast - gcol)
    s = s_sc[...]
    v_new = u - bdot(w, s, 'hck,hkv->hcv')
    o = bdot(qd, s, 'hck,hkv->hcv') + bdot(a, v_new, 'hcs,hsv->hcv')
    s_new = s * jnp.exp(glast) + bdot(kd, v_new, 'hck,hcv->hkv')
    s_sc[...] = s_new

    @pl.when(n == pl.num_programs(2) - 1)
    def _():
        so_ref[0] = s_new

    on = o * lax.rsqrt(jnp.mean(o * o, axis=-1, keepdims=True) + RMS_EPS) * nw_ref[...][None]
    for h in heads:
        sl = slice(h * LANE, (h + 1) * LANE)
        o_ref[:, sl] = (on[h] * _silu(z_ref[:, sl])).astype(o_ref.dtype)


def _gdn(x, conv_w, gb, norm_w, hist, s0, nb, tseq, c, hb, nheads):
    m = x.shape[0]
    n = tseq // c
    w = hb * LANE
    ng = nheads // hb
    kw = nheads * LANE
    taps = conv_w.shape[0]
    row_col = lambda cb: pl.BlockSpec((c, w), lambda b, hg, i, cb=cb: (b * n + i, cb + hg))
    wcol = lambda cb: pl.BlockSpec((taps, w), lambda b, hg, i, cb=cb: (0, cb + hg))
    in_specs = [row_col(0), row_col(kw // w), row_col(2 * kw // w),
                wcol(0), wcol(kw // w), wcol(2 * kw // w),
                row_col(3 * kw // w),
                pl.BlockSpec((1, c, LANE), lambda b, hg, i: (hg, b * n + i, 0)),
                pl.BlockSpec((1, LANE), lambda b, hg, i: (0, 0))]
    args = [x, x, x, conv_w, conv_w, conv_w, x, gb, norm_w.reshape(1, LANE)]
    state_spec = pl.BlockSpec((1, hb, LANE, LANE), lambda b, hg, i: (b, hg, 0, 0))
    if s0 is not None:
        hcol = lambda cb: pl.BlockSpec((SUBLANE, w), lambda b, hg, i, cb=cb: (b, cb + hg))
        in_specs += [hcol(0), hcol(kw // w), hcol(2 * kw // w), state_spec]
        args += [hist, hist, hist, s0]
    return pl.pallas_call(
        functools.partial(_gdn_kernel, hb=hb, c=c, taps=taps, use_state=s0 is not None, scale=LANE ** -0.5),
        grid=(nb, ng, n),
        in_specs=in_specs,
        out_specs=[pl.BlockSpec((c, w), lambda b, hg, i: (b * n + i, hg)), state_spec],
        out_shape=[jax.ShapeDtypeStruct((m, kw), BF16), jax.ShapeDtypeStruct((nb, nheads, LANE, LANE), F32)],
        scratch_shapes=[pltpu.VMEM((hb, LANE, LANE), F32), pltpu.VMEM((3, SUBLANE, w), F32)],
        compiler_params=_params("parallel", "parallel", "arbitrary"),
        name="gdn",
    )(*args)


def _merge_kernel(of_ref, og_ref, wf_ref, wg_ref, gf_ref, gg_ref, o_ref):
    a = jnp.dot(of_ref[...], wf_ref[...].astype(BF16), preferred_element_type=F32)
    b = jnp.dot(og_ref[...], wg_ref[...].astype(BF16), preferred_element_type=F32)
    o_ref[...] = (jax.nn.sigmoid(gf_ref[...]) * a + jax.nn.sigmoid(gg_ref[...]) * b).astype(o_ref.dtype)


def _merge(o_fox, o_gdn, w_bf, w_bg, proj, gf_col0, gg_col0, tm, tn):
    m, kf = o_fox.shape
    kg = o_gdn.shape[1]
    n = w_bf.shape[1]
    return pl.pallas_call(
        _merge_kernel,
        grid=(m // tm, n // tn),
        in_specs=[pl.BlockSpec((tm, kf), lambda i, j: (i, 0)),
                  pl.BlockSpec((tm, kg), lambda i, j: (i, 0)),
                  pl.BlockSpec((kf, tn), lambda i, j: (0, j)),
                  pl.BlockSpec((kg, tn), lambda i, j: (0, j)),
                  pl.BlockSpec((tm, tn), lambda i, j: (i, gf_col0 // tn + j)),
                  pl.BlockSpec((tm, tn), lambda i, j: (i, gg_col0 // tn + j))],
        out_specs=pl.BlockSpec((tm, tn), lambda i, j: (i, j)),
        out_shape=jax.ShapeDtypeStruct((m, n), BF16),
        compiler_params=_params("parallel", "arbitrary"),
        name="merge",
    )(o_fox, o_gdn, w_bf, w_bg, proj, proj)


def _resid_kernel(a_ref, w_ref, x_ref, gt_ref, o_ref):
    acc = jnp.dot(a_ref[...], w_ref[...].astype(BF16), preferred_element_type=F32)
    o_ref[0] = x_ref[0] + _rows(acc, gt_ref[0]) * acc


def _resid_matmul(a, w, x3, gt3, tiles_per_mod, tn, name):
    nt, tm, n = x3.shape
    k = a.shape[1]
    mr = gt3.shape[1]
    return pl.pallas_call(
        _resid_kernel,
        grid=(nt, n // tn),
        in_specs=[pl.BlockSpec((tm, k), lambda i, j: (i, 0)),
                  pl.BlockSpec((k, tn), lambda i, j: (0, j)),
                  pl.BlockSpec((1, tm, tn), lambda i, j: (i, 0, j)),
                  pl.BlockSpec((1, mr, tn), lambda i, j: (i // tiles_per_mod, 0, j))],
        out_specs=pl.BlockSpec((1, tm, tn), lambda i, j: (i, 0, j)),
        out_shape=jax.ShapeDtypeStruct((nt, tm, n), F32),
        compiler_params=_params("parallel", "arbitrary"),
        name=name,
    )(a, w, x3, gt3)


def _ffn_up_kernel(*refs, stride, taps, tiles_per_seq, halo_from_h):
    if halo_from_h:
        hh_ref, h_ref, wg_ref, wu_ref, cw_ref, act_ref, tail_ref, hcat = refs
    else:
        hist_ref, h_ref, wg_ref, wu_ref, cw_ref, act_ref, tail_ref = refs
    i, j = pl.program_id(0), pl.program_id(1)
    tm = h_ref.shape[0]
    if halo_from_h:
        hh = hh_ref.shape[0]

        @pl.when(j == 0)
        def _():
            hcat[0:hh, :] = jnp.where(i % tiles_per_seq == 0, jnp.zeros_like(hh_ref[...]), hh_ref[...])
            hcat[hh:hh + tm, :] = h_ref[...]

        ext = jnp.dot(hcat[...], wg_ref[...].astype(BF16), preferred_element_type=F32)
    else:
        hh = hist_ref.shape[0]
        ext = jnp.concatenate(
            [hist_ref[...], jnp.dot(h_ref[...], wg_ref[...].astype(BF16), preferred_element_type=F32)], axis=0)
    up = jnp.dot(h_ref[...], wu_ref[...].astype(BF16), preferred_element_type=F32)
    y = None
    for tap in range(taps):
        off = hh - (taps - 1 - tap) * stride
        term = ext[off:off + tm] * cw_ref[tap:tap + 1, :]
        y = term if y is None else y + term
    act_ref[...] = (_silu(y) * up).astype(act_ref.dtype)
    tail_ref[0] = ext[hh + tm - tail_ref.shape[1]:hh + tm]


def _ffn_up(h, hist, w_gate, w_up, conv_w, tm, tn, stride, tiles_per_seq, tail_rows):
    m, k = h.shape
    n = w_gate.shape[1]
    taps = conv_w.shape[0]
    halo_from_h = hist is None
    if halo_from_h:
        hh = BF16_ROWS
        first = pl.BlockSpec((hh, k), lambda i, j: (jnp.maximum(i * (tm // hh) - 1, 0), 0))
        first_arg = h
        scratch = [pltpu.VMEM((hh + tm, k), BF16)]
    else:
        hh = hist.shape[0]
        first = pl.BlockSpec((hh, tn), lambda i, j: (0, j))
        first_arg = hist
        scratch = []
    return pl.pallas_call(
        functools.partial(_ffn_up_kernel, stride=stride, taps=taps, tiles_per_seq=tiles_per_seq,
                          halo_from_h=halo_from_h),
        grid=(m // tm, n // tn),
        in_specs=[first,
                  pl.BlockSpec((tm, k), lambda i, j: (i, 0)),
                  pl.BlockSpec((k, tn), lambda i, j: (0, j)),
                  pl.BlockSpec((k, tn), lambda i, j: (0, j)),
                  pl.BlockSpec((taps, tn), lambda i, j: (0, j))],
        out_specs=[pl.BlockSpec((tm, tn), lambda i, j: (i, j)),
                   pl.BlockSpec((1, tail_rows, tn), lambda i, j: (i, 0, j))],
        out_shape=[jax.ShapeDtypeStruct((m, n), BF16), jax.ShapeDtypeStruct((m // tm, tail_rows, n), F32)],
        scratch_shapes=scratch,
        compiler_params=_params("parallel", "arbitrary"),
        name="ffn_up",
    )(first_arg, h, w_gate, w_up, conv_w)


def _layer(x2, mods, wts, lay, *, prompt, nb, tseq, fox_fn, gdn_s0, gdn_hist, ffn_hist):
    m, d = x2.shape
    sh_m, sc_m, gt_m, sh_f, sc_f, gt_f = mods
    nh, nkv, g, gh = lay["nh"], lay["nkv"], lay["g"], lay["gh"]
    kw = gh * LANE
    if prompt:
        tr = _pick(tseq, 256, SUBLANE)
        tm = _pick(tseq, 1024, LANE)
        stride = 1
    else:
        tr = nb
        tm = m
        stride = nb
    tiles_mod = (tseq // tr) if prompt else m // tr
    h = _normmod(x2.reshape(m // tr, tr, d), wts["norm_mix"], sc_m, sh_m, tiles_mod).reshape(m, d)
    mm = lambda w: _matmul(h, w, tm, _pick(w.shape[1], 512, LANE))
    proj_fox, proj_gdn, proj_gate = mm(wts["w_fox"]), mm(wts["w_gdn"]), mm(wts["w_gate"])
    small = _small_heads(h, wts["w_small"], wts["p_small"], nh, tm)
    qn, kn, vb, vt, k_f, v_f = _qknorm(proj_fox, wts["fox_q_norm"], wts["fox_k_norm"], nh, nkv, _pick(m, 256, LANE))
    o_fox = fox_fn(qn, kn, vb, vt, small)
    hb = lay["hb"]
    if prompt:
        o_gdn, s_fin = _gdn(proj_gdn, wts["gdn_conv_w"], _group_gb(small, nh, gh, hb), wts["gdn_norm"], None, None,
                            nb, tseq, lay["chunk"], hb, gh)
    else:
        ts = m // nb
        cpad = lay["cpad"]
        to_bm = lambda a: jnp.pad(a.reshape(ts, nb, -1).transpose(1, 0, 2),
                                  ((0, 0), (0, cpad - ts), (0, 0))).reshape(nb * cpad, -1)
        o_bm, s_fin = _gdn(to_bm(proj_gdn), wts["gdn_conv_w"], _group_gb(to_bm(small), nh, gh, hb), wts["gdn_norm"],
                           gdn_hist, gdn_s0, nb, cpad, cpad, hb, gh)
        o_gdn = o_bm.reshape(nb, cpad, kw)[:, :ts].transpose(1, 0, 2).reshape(m, kw)
    tn = _pick(d, 512, LANE)
    merged = _merge(o_fox, o_gdn, wts["w_branch_fox"], wts["w_branch_gdn"], proj_gate, 0, d, tm, tn)
    tiles_mod_m = (tseq // tm) if prompt else 1
    x3 = _resid_matmul(merged, wts["w_out"], x2.reshape(m // tm, tm, d), gt_m, tiles_mod_m, tn, "out_proj")
    x2 = x3.reshape(m, d)
    h2 = _normmod(x2.reshape(m // tr, tr, d), wts["norm_ffn"], sc_f, sh_f, tiles_mod).reshape(m, d)
    dff = wts["ffn_w_gate"].shape[1]
    tnf = _pick(dff, 256, LANE)
    act, tail = _ffn_up(h2, ffn_hist, wts["ffn_w_gate"], wts["ffn_w_up"], wts["ffn_conv_w"], tm, tnf, stride,
                        tseq // tm if prompt else 1, SUBLANE if prompt else m)
    tm2 = _pick(tm, 512, LANE) if prompt else m
    y3 = _resid_matmul(act, wts["ffn_w_down"], x2.reshape(m // tm2, tm2, d), gt_f,
                       (tseq // tm2) if prompt else 1, _pick(d, 256, LANE), "ffn_down")
    return y3.reshape(m, d), dict(gdn=proj_gdn, small=small, k=k_f, v=v_f, s=s_fin, tail=tail)


def _group_gb(small, nh, gh, hb):
    rows = small.shape[0]
    gg = small[:, nh:nh + gh].reshape(rows, gh // hb, hb)
    bb = small[:, nh + gh:nh + 2 * gh].reshape(rows, gh // hb, hb)
    gb = jnp.concatenate([gg, bb], axis=-1).transpose(1, 0, 2)
    return jnp.pad(gb, ((0, 0), (0, 0), (0, LANE - 2 * hb)))


def kernel(x_prompt, x_sample, cache_k, cache_v, cache_logf, state_gdn, state_gdn_conv, state_ffn_conv, page_table, c_prompt, c_sample, w_ada, b_ada, norm_mix, norm_ffn, w_in, fox_b_f, fox_q_norm, fox_k_norm, gdn_conv_w, gdn_A_log, gdn_dt_bias, gdn_norm, w_branch_fox, w_branch_gdn, w_out, ffn_w_gate, ffn_w_up, ffn_conv_w, ffn_w_down):
    nb, t, d = x_prompt.shape
    bs, ts, _ = x_sample.shape
    depth = w_in.shape[0]
    page, nkv = cache_k.shape[2], cache_k.shape[3]
    nh = cache_logf.shape[-1]
    g = nh // nkv
    gh = state_gdn.shape[2]
    kw = gh * LANE
    gconv = gdn_conv_w.shape[1]
    fconv = ffn_conv_w.shape[1]
    dff = ffn_w_gate.shape[-1]
    assert cache_k.shape[-1] == LANE and state_gdn.shape[-1] == LANE and state_gdn.shape[-2] == LANE
    assert gh == nh and ts >= gconv - 1 and ts >= fconv - 1

    sizes = (nh * LANE, nkv * LANE, nkv * LANE, nh, 3 * kw, kw, gh, gh, d, d)
    starts = [0]
    for s_ in sizes:
        starts.append(starts[-1] + s_)
    assert starts[-1] == w_in.shape[-1]
    hb = _pick(gh, 16, 1)
    lay = dict(nh=nh, nkv=nkv, g=g, gh=gh, hb=hb, chunk=_pick(t, 64, SUBLANE), cpad=SUBLANE)

    xs = x_sample.transpose(1, 0, 2).reshape(ts * bs, d)
    xp = x_prompt.reshape(nb * t, d)
    mc = -(-nb // SUBLANE) * SUBLANE
    c_all = jnp.concatenate([c_prompt, jnp.zeros((mc - nb, d), F32), c_sample], axis=0)

    outs_p = [[] for _ in range(6)]
    outs_s = [[] for _ in range(6)]
    for l in range(depth):
        wi = w_in[l]
        kin = w_in.shape[1]
        w_in2 = w_in.reshape(depth * kin, w_in.shape[2])
        run = lambda a, b: _wprep(w_in2, l * kin, kin, starts[a], starts[b] - starts[a])
        w_small = jnp.concatenate([wi[:, starts[3]:starts[4]], wi[:, starts[6]:starts[7]], wi[:, starts[7]:starts[8]]],
                                  axis=1)
        w_small = jnp.pad(w_small, ((0, 0), (0, LANE - 3 * nh))).astype(BF16)
        zpad = jnp.zeros((LANE - 2 * nh,), F32)
        p_small = jnp.stack([jnp.concatenate([fox_b_f[l], gdn_dt_bias[l], zpad]),
                             jnp.concatenate([jnp.zeros((nh,), F32), gdn_A_log[l], zpad])])
        p_small = jnp.pad(p_small, ((0, SUBLANE - 2), (0, 0)))
        wts = dict(norm_mix=norm_mix[l], norm_ffn=norm_ffn[l], w_fox=run(0, 3), w_gdn=run(4, 6), w_gate=run(8, 10),
                   w_small=w_small, p_small=p_small,
                   fox_q_norm=fox_q_norm[l], fox_k_norm=fox_k_norm[l], gdn_conv_w=gdn_conv_w[l],
                   gdn_norm=gdn_norm[l], w_branch_fox=w_branch_fox[l], w_branch_gdn=w_branch_gdn[l],
                   w_out=w_out[l], ffn_w_gate=ffn_w_gate[l], ffn_w_up=ffn_w_up[l],
                   ffn_conv_w=ffn_conv_w[l], ffn_w_down=ffn_w_down[l].astype(BF16))

        mod = _ada(c_all, w_ada[l], b_ada[l])
        mods_p = [mod[:nb, i * d:(i + 1) * d].reshape(nb, 1, d) for i in range(6)]
        mods_s = [mod[mc:, i * d:(i + 1) * d].reshape(1, bs, d) for i in range(6)]

        def fox_p(qn, kn, vb, vt, small):
            f_tm = _cumf(small, nb, t)
            fcol = f_tm[:, :nh].reshape(nb, t, nkv, g).transpose(0, 2, 1, 3)
            fcol = jnp.pad(fcol, ((0, 0), (0, 0), (0, 0), (0, LANE - g)))
            return _fox_prompt(qn, kn, vt, fcol, nb, t, nkv, g)

        xp, st = _layer(xp, mods_p, wts, lay, prompt=True, nb=nb, tseq=t, fox_fn=fox_p, gdn_s0=None,
                        gdn_hist=None, ffn_hist=None)
        outs_p[0].append(st["k"].reshape(nb, t, nkv, LANE))
        outs_p[1].append(st["v"].reshape(nb, t, nkv, LANE))
        outs_p[2].append(st["small"].reshape(nb, t, LANE)[:, :, :nh])
        outs_p[3].append(st["s"])
        outs_p[4].append(st["gdn"].reshape(nb, t, -1)[:, t - (gconv - 1):, :3 * kw])
        tail = st["tail"].reshape(nb, -1, SUBLANE, dff)
        outs_p[5].append(tail[:, -1, SUBLANE - (fconv - 1):, :])

        n_pool = cache_k.shape[1]
        kc = cache_k.reshape(depth * n_pool, page, nkv, LANE)
        vc = cache_v.reshape(depth * n_pool, page, nkv, LANE)
        lfc = cache_logf.reshape(depth * n_pool, page, nh)
        page_ids = page_table + l * n_pool

        def fox_s(qn, kn, vb, vt, small):
            bm = lambda a, n: a.reshape(ts, bs, n, LANE).transpose(1, 0, 2, 3).reshape(bs, ts * n, LANE)
            nn = -(-ts * nkv // BF16_ROWS) * BF16_ROWS
            padr = lambda a, n: jnp.pad(a, ((0, 0), (0, n - a.shape[1]), (0, 0)))
            lfn = padr(small.reshape(ts, bs, LANE).transpose(1, 0, 2), BF16_ROWS)
            o = _fox_sample(page_ids, bm(qn, nh), padr(bm(kn, nkv), nn), padr(bm(vb, nkv), nn), lfn, kc, vc, lfc,
                            nkv, g, ts)
            return o.reshape(bs, ts, nh, LANE).transpose(1, 0, 2, 3).reshape(ts * bs, nh * LANE)

        ghist = jnp.pad(state_gdn_conv[l], ((0, 0), (SUBLANE - (gconv - 1), 0), (0, 0))).reshape(bs * SUBLANE, 3 * kw)
        fhist = state_ffn_conv[l].transpose(1, 0, 2).reshape((fconv - 1) * bs, dff)
        xs, st = _layer(xs, mods_s, wts, lay, prompt=False, nb=bs, tseq=ts, fox_fn=fox_s, gdn_s0=state_gdn[l],
                        gdn_hist=ghist, ffn_hist=fhist)
        bm3 = lambda a: a.reshape(ts, bs, -1).transpose(1, 0, 2)
        outs_s[0].append(bm3(st["k"]).reshape(bs, ts, nkv, LANE))
        outs_s[1].append(bm3(st["v"]).reshape(bs, ts, nkv, LANE))
        outs_s[2].append(bm3(st["small"])[:, :, :nh])
        outs_s[3].append(st["s"])
        graw = bm3(st["gdn"])[:, :, :3 * kw]
        outs_s[4].append(jnp.concatenate([state_gdn_conv[l], graw], axis=1)[:, -(gconv - 1):])
        fraw = bm3(st["tail"].reshape(ts * bs, dff))
        outs_s[5].append(jnp.concatenate([state_ffn_conv[l], fraw], axis=1)[:, -(fconv - 1):])

    y_prompt = xp.reshape(nb, t, d)
    y_sample = xs.reshape(ts, bs, d).transpose(1, 0, 2)
    return (y_prompt, y_sample, *(jnp.stack(a) for a in outs_p), *(jnp.stack(a) for a in outs_s))
```

```python
import functools

import jax
import jax.numpy as jnp
from jax import lax
from jax.experimental import pallas as pl
from jax.experimental.pallas import tpu as pltpu

F32 = jnp.float32
BF16 = jnp.bfloat16
RMS_EPS = 1e-6
L2_EPS = 1e-6
NEG_INF = -1e30
LANE = 128
SUBLANE = 8
BF16_ROWS = 16
VMEM_LIMIT_BYTES = 56 * 2 ** 20

NN = (((1,), (0,)), ((), ()))
NT = (((1,), (1,)), ((), ()))
TN = (((0,), (0,)), ((), ()))


def _pick(n, pref, align):
    t = min(pref, n) // align * align
    while t >= align:
        if n % t == 0:
            return t
        t -= align
    return n


def _params(*sem):
    return pltpu.CompilerParams(dimension_semantics=sem, vmem_limit_bytes=VMEM_LIMIT_BYTES)


def _dot(a, b, dims=NN):
    return lax.dot_general(a.astype(BF16), b.astype(BF16), dims, preferred_element_type=F32)


def _dot_sel(sel, x, sel_first=True, dims=NN):
    h = x.astype(BF16)
    r = x - h.astype(F32)
    m = r.astype(BF16)
    l = (r - m.astype(F32)).astype(BF16)
    if sel_first:
        d = lambda p: lax.dot_general(sel, p, dims, preferred_element_type=F32)
    else:
        d = lambda p: lax.dot_general(p, sel, dims, preferred_element_type=F32)
    return d(h) + (d(m) + d(l))


def _silu(x):
    return x * jax.nn.sigmoid(x)


def _softplus(x):
    return jnp.maximum(x, 0.0) + jnp.log1p(jnp.exp(-jnp.abs(x)))


def _rows(a, m):
    if m.shape[0] == 1 or m.shape[0] == a.shape[0]:
        return m
    reps = a.shape[0] // m.shape[0]
    return jnp.concatenate([m] * reps, axis=0)


def _ada_kernel(c_ref, w_ref, b_ref, o_ref):
    a = _silu(c_ref[...])
    o_ref[...] = _dot(a, w_ref[...]) + b_ref[...]


def _ada(c_all, w_ada, b_ada):
    mc, d = c_all.shape
    n = w_ada.shape[1]
    tn = _pick(n, 512, LANE)
    return pl.pallas_call(
        _ada_kernel,
        grid=(n // tn,),
        in_specs=[pl.BlockSpec((mc, d), lambda j: (0, 0)),
                  pl.BlockSpec((d, tn), lambda j: (0, j)),
                  pl.BlockSpec((1, tn), lambda j: (0, j))],
        out_specs=pl.BlockSpec((mc, tn), lambda j: (0, j)),
        out_shape=jax.ShapeDtypeStruct((mc, n), F32),
        compiler_params=_params("parallel"),
        name="ada",
    )(c_all, w_ada, b_ada.reshape(1, n))


def _normmod_kernel(x_ref, g_ref, sc_ref, sh_ref, o_ref):
    x = x_ref[0]
    y = x * lax.rsqrt(jnp.mean(x * x, axis=-1, keepdims=True) + RMS_EPS) * g_ref[...]
    o_ref[0] = (y * (1.0 + _rows(y, sc_ref[0])) + _rows(y, sh_ref[0])).astype(o_ref.dtype)


def _normmod(x3, gain, sc3, sh3, tiles_per_mod):
    nt, tr, d = x3.shape
    mr = sc3.shape[1]
    mod_spec = pl.BlockSpec((1, mr, d), lambda i: (i // tiles_per_mod, 0, 0))
    return pl.pallas_call(
        _normmod_kernel,
        grid=(nt,),
        in_specs=[pl.BlockSpec((1, tr, d), lambda i: (i, 0, 0)),
                  pl.BlockSpec((1, d), lambda i: (0, 0)),
                  mod_spec, mod_spec],
        out_specs=pl.BlockSpec((1, tr, d), lambda i: (i, 0, 0)),
        out_shape=jax.ShapeDtypeStruct((nt, tr, d), BF16),
        compiler_params=_params("parallel"),
        name="normmod",
    )(x3, gain.reshape(1, d), sc3, sh3)


def _wprep_kernel(a_ref, b_ref, o_ref, *, delta):
    a = a_ref[...]
    if delta:
        a = jnp.concatenate([a[delta:], b_ref[0:delta, :]], axis=0)
    o_ref[...] = a.astype(BF16)


def _wprep(wt, row0, rows):
    k = wt.shape[1]
    tr = _pick(rows, 512, LANE)
    tk = _pick(k, 2048, LANE)
    delta, rb = row0 % tr, row0 // tr
    assert delta < LANE
    nxt = (lambda i, j: ((rb + i + 1) * (tr // LANE), j)) if delta else (lambda i, j: (0, j))
    return pl.pallas_call(
        functools.partial(_wprep_kernel, delta=delta),
        grid=(rows // tr, k // tk),
        in_specs=[pl.BlockSpec((tr, tk), lambda i, j: (rb + i, j)), pl.BlockSpec((LANE, tk), nxt)],
        out_specs=pl.BlockSpec((tr, tk), lambda i, j: (i, j)),
        out_shape=jax.ShapeDtypeStruct((rows, k), BF16),
        compiler_params=_params("parallel", "parallel"),
        name="wprep",
    )(wt, wt)


def _wsmall_kernel(a_ref, b_ref, o_ref, *, nh):
    row = lax.broadcasted_iota(jnp.int32, a_ref.shape, 0)
    o_ref[...] = jnp.where(row < nh, a_ref[...], jnp.where(row < 3 * nh, b_ref[...], 0.0)).astype(BF16)


def _wsmall(wt, row_f, row_ab, nh):
    assert row_f % LANE == 0 and row_ab % LANE == nh and 3 * nh <= LANE
    k = wt.shape[1]
    tk = _pick(k, 2048, LANE)
    return pl.pallas_call(
        functools.partial(_wsmall_kernel, nh=nh),
        grid=(k // tk,),
        in_specs=[pl.BlockSpec((LANE, tk), lambda j: (row_f // LANE, j)),
                  pl.BlockSpec((LANE, tk), lambda j: (row_ab // LANE, j))],
        out_specs=pl.BlockSpec((LANE, tk), lambda j: (0, j)),
        out_shape=jax.ShapeDtypeStruct((LANE, k), BF16),
        compiler_params=_params("parallel"),
        name="wsmall",
    )(wt, wt)


def _mm_kernel(x_ref, w_ref, o_ref):
    o_ref[...] = lax.dot_general(x_ref[...], w_ref[...], NT, preferred_element_type=F32).astype(o_ref.dtype)


def _matmul(x, wt, tm, tn, out_dtype=F32):
    m, k = x.shape
    n = wt.shape[0]
    return pl.pallas_call(
        _mm_kernel,
        grid=(m // tm, n // tn),
        in_specs=[pl.BlockSpec((tm, k), lambda i, j: (i, 0)),
                  pl.BlockSpec((tn, k), lambda i, j: (j, 0))],
        out_specs=pl.BlockSpec((tm, tn), lambda i, j: (i, j)),
        out_shape=jax.ShapeDtypeStruct((m, n), out_dtype),
        compiler_params=_params("parallel", "arbitrary"),
        name="inproj",
    )(x, wt)


def _small_kernel(x_ref, w_ref, p_ref, o_ref, *, nh):
    acc = lax.dot_general(x_ref[...], w_ref[...], NT, preferred_element_type=F32)
    xb = acc + p_ref[0:1, :]
    lane = lax.broadcasted_iota(jnp.int32, acc.shape, 1)
    sp = _softplus(xb)
    logf = jnp.minimum(xb, 0.0) - jnp.log1p(jnp.exp(-jnp.abs(xb)))
    g = -jnp.exp(p_ref[1:2, :]) * sp
    beta = jax.nn.sigmoid(acc)
    o_ref[...] = jnp.where(lane < nh, logf, jnp.where(lane < 2 * nh, g, jnp.where(lane < 3 * nh, beta, 0.0)))


def _small_heads(h, w_small, p_small, nh, tm):
    m, k = h.shape
    return pl.pallas_call(
        functools.partial(_small_kernel, nh=nh),
        grid=(m // tm,),
        in_specs=[pl.BlockSpec((tm, k), lambda i: (i, 0)),
                  pl.BlockSpec((LANE, k), lambda i: (0, 0)),
                  pl.BlockSpec((SUBLANE, LANE), lambda i: (0, 0))],
        out_specs=pl.BlockSpec((tm, LANE), lambda i: (i, 0)),
        out_shape=jax.ShapeDtypeStruct((m, LANE), F32),
        compiler_params=_params("parallel"),
        name="small_heads",
    )(h, w_small, p_small)


def _qknorm_kernel(x_ref, qg_ref, kg_ref, qn_ref, kn_ref, vb_ref, vt_ref, kf_ref, vf_ref, *, nq, nkv):
    def norm(x, g):
        return x * lax.rsqrt(jnp.mean(x * x, axis=-1, keepdims=True) + RMS_EPS) * g

    for h in range(nq):
        sl = slice(h * LANE, (h + 1) * LANE)
        qn_ref[:, sl] = norm(x_ref[:, sl], qg_ref[...]).astype(BF16)
    for h in range(nkv):
        sl = slice(h * LANE, (h + 1) * LANE)
        kx = norm(x_ref[:, (nq + h) * LANE:(nq + h + 1) * LANE], kg_ref[...])
        kf_ref[:, sl] = kx
        kn_ref[:, sl] = kx.astype(BF16)
        vx = x_ref[:, (nq + nkv + h) * LANE:(nq + nkv + h + 1) * LANE]
        vf_ref[:, sl] = vx
        vb_ref[:, sl] = vx.astype(BF16)
        vt_ref[sl, :] = vx.T.astype(BF16)


def _qknorm(proj, q_gain, k_gain, nq, nkv, tm):
    m = proj.shape[0]
    wq, wk = nq * LANE, nkv * LANE
    row = lambda w: pl.BlockSpec((tm, w), lambda i: (i, 0))
    gain = pl.BlockSpec((1, LANE), lambda i: (0, 0))
    return pl.pallas_call(
        functools.partial(_qknorm_kernel, nq=nq, nkv=nkv),
        grid=(m // tm,),
        in_specs=[row(wq + 2 * wk), gain, gain],
        out_specs=[row(wq), row(wk), row(wk), pl.BlockSpec((wk, tm), lambda i: (0, i)), row(wk), row(wk)],
        out_shape=[jax.ShapeDtypeStruct((m, wq), BF16), jax.ShapeDtypeStruct((m, wk), BF16),
                   jax.ShapeDtypeStruct((m, wk), BF16), jax.ShapeDtypeStruct((wk, m), BF16),
                   jax.ShapeDtypeStruct((m, wk), F32), jax.ShapeDtypeStruct((m, wk), F32)],
        compiler_params=_params("parallel"),
        name="qknorm",
    )(proj, q_gain.reshape(1, LANE), k_gain.reshape(1, LANE))


def _cumf_kernel(x_ref, o_ref, *, blk):
    t = x_ref.shape[0]
    r = lax.broadcasted_iota(jnp.int32, (blk, blk), 0)
    c = lax.broadcasted_iota(jnp.int32, (blk, blk), 1)
    tri = (r >= c).astype(BF16)
    carry = jnp.zeros((1, LANE), F32)
    for i in range(t // blk):
        cum = _dot_sel(tri, x_ref[i * blk:(i + 1) * blk, :]) + carry
        carry = cum[blk - 1:blk, :]
        o_ref[i * blk:(i + 1) * blk, :] = cum


def _cumf(small, nb, t):
    blk = _pick(t, 256, LANE)
    return pl.pallas_call(
        functools.partial(_cumf_kernel, blk=blk),
        grid=(nb,),
        in_specs=[pl.BlockSpec((t, LANE), lambda b: (b, 0))],
        out_specs=pl.BlockSpec((t, LANE), lambda b: (b, 0)),
        out_shape=jax.ShapeDtypeStruct((nb * t, LANE), F32),
        compiler_params=_params("parallel"),
        name="cumf",
    )(small)


def _fox_prompt_kernel(q_ref, k_ref, vt_ref, f_ref, o_ref, m_sc, l_sc, acc_sc, *, g, tq, scale):
    qi = pl.program_id(2)
    m_sc[...] = jnp.full(m_sc.shape, NEG_INF, F32)
    l_sc[...] = jnp.zeros(l_sc.shape, F32)
    acc_sc[...] = jnp.zeros(acc_sc.shape, F32)
    qs = jnp.concatenate([q_ref[:, i * LANE:(i + 1) * LANE] for i in range(g)], axis=0)

    def block(ki, masked):
        ks = pl.ds(pl.multiple_of(ki * tq, tq), tq)
        st = lax.dot_general(k_ref[ks, :], qs, NT, preferred_element_type=F32) * scale
        st = st - jnp.concatenate([jnp.broadcast_to(f_ref[0, 0, ks, i:i + 1], (tq, tq)) for i in range(g)], axis=1)
        if masked:
            keep = lax.broadcasted_iota(jnp.int32, (tq, tq), 0) <= lax.broadcasted_iota(jnp.int32, (tq, tq), 1)
            st = jnp.where(jnp.concatenate([keep] * g, axis=1), st, NEG_INF)
        m_prev = m_sc[...]
        m_new = jnp.maximum(m_prev, jnp.max(st, axis=0, keepdims=True))
        alpha = jnp.exp(m_prev - m_new)
        p = jnp.exp(st - m_new)
        l_sc[...] = alpha * l_sc[...] + jnp.sum(p, axis=0, keepdims=True)
        acc_sc[...] = alpha * acc_sc[...] + jnp.dot(vt_ref[:, ks], p.astype(BF16), preferred_element_type=F32)
        m_sc[...] = m_new

    def body(ki, carry):
        block(ki, False)
        return carry

    lax.fori_loop(0, qi, body, 0)
    block(qi, True)
    ot = acc_sc[...] / l_sc[...]
    for i in range(g):
        o_ref[:, i * LANE:(i + 1) * LANE] = ot[:, i * tq:(i + 1) * tq].T.astype(o_ref.dtype)


def _fox_prompt(qn, kn, vt, fcol, nb, t, nkv, g):
    tq = _pick(t, 512, LANE)
    nq = t // tq
    n = g * tq
    return pl.pallas_call(
        functools.partial(_fox_prompt_kernel, g=g, tq=tq, scale=LANE ** -0.5),
        grid=(nb, nkv, nq),
        in_specs=[pl.BlockSpec((tq, g * LANE), lambda b, h, qi: (b * nq + qi, h)),
                  pl.BlockSpec((t, LANE), lambda b, h, qi: (b, h)),
                  pl.BlockSpec((LANE, t), lambda b, h, qi: (h, b)),
                  pl.BlockSpec((1, 1, t, LANE), lambda b, h, qi: (b, h, 0, 0))],
        out_specs=pl.BlockSpec((tq, g * LANE), lambda b, h, qi: (b * nq + qi, h)),
        out_shape=jax.ShapeDtypeStruct((nb * t, nkv * g * LANE), BF16),
        scratch_shapes=[pltpu.VMEM((1, n), F32), pltpu.VMEM((1, n), F32), pltpu.VMEM((LANE, n), F32)],
        compiler_params=_params("parallel", "parallel", "arbitrary"),
        name="fox_prompt",
    )(qn, kn, vt, fcol)


def _fox_sample_kernel(pt_ref, q_ref, kn_ref, vn_ref, lfn_ref, *refs, npg, nkv, g, ts, scale):
    k_refs, v_refs, lf_refs = refs[:npg], refs[npg:2 * npg], refs[2 * npg:3 * npg]
    o_ref, m_sc, l_sc, acc_sc, f_sc = refs[3 * npg:]
    p = pl.program_id(1)
    page = k_refs[0].shape[1]
    nh = nkv * g
    rows = ts * nh
    cols = page * nkv

    @pl.when(p == 0)
    def _():
        m_sc[...] = jnp.full(m_sc.shape, NEG_INF, F32)
        l_sc[...] = jnp.zeros(l_sc.shape, F32)
        acc_sc[...] = jnp.zeros(acc_sc.shape, F32)
        f_sc[...] = jnp.zeros(f_sc.shape, F32)

    q = q_ref[0]

    def update(s, pv):
        m_prev = m_sc[...]
        m_new = jnp.maximum(m_prev, jnp.max(s, axis=-1, keepdims=True))
        alpha = jnp.exp(m_prev - m_new)
        pr = jnp.exp(s - m_new)
        l_sc[...] = alpha * l_sc[...] + jnp.sum(pr, axis=-1, keepdims=True)
        acc_sc[...] = alpha * acc_sc[...] + pv(pr.astype(BF16))
        m_sc[...] = m_new

    def own_kv(width):
        rq = lax.broadcasted_iota(jnp.int32, (rows, width), 0)
        cq = lax.broadcasted_iota(jnp.int32, (rows, width), 1)
        return rq, cq, (cq % nkv) == ((rq % nh) // g)

    ue = (lax.broadcasted_iota(jnp.int32, (page, cols), 0)
          <= lax.broadcasted_iota(jnp.int32, (page, cols), 1) // nkv).astype(BF16)
    _, _, own = own_kv(cols)
    carry = f_sc[...]
    parts = []
    for i in range(npg):
        fexp = _dot_sel(ue, lf_refs[i][0], sel_first=False, dims=TN) + carry[:, 0:1]
        carry = jnp.broadcast_to(fexp[:, cols - 1:cols], carry.shape)
        kf = k_refs[i][0].reshape(cols, LANE).astype(BF16)
        s = lax.dot_general(q, kf, NT, preferred_element_type=F32) * scale - jnp.concatenate([fexp] * ts, axis=0)
        parts.append(jnp.where(own, s, NEG_INF))
    f_sc[...] = carry

    def pv_pages(pr):
        out = None
        for i in range(npg):
            t_ = jnp.dot(pr[:, i * cols:(i + 1) * cols], v_refs[i][0].reshape(cols, LANE).astype(BF16),
                         preferred_element_type=F32)
            out = t_ if out is None else out + t_
        return out

    update(jnp.concatenate(parts, axis=1), pv_pages)

    @pl.when(p == pl.num_programs(1) - 1)
    def _():
        nn = kn_ref.shape[1]
        nl = lfn_ref.shape[1]
        uen = (lax.broadcasted_iota(jnp.int32, (nl, nn), 0)
               <= lax.broadcasted_iota(jnp.int32, (nl, nn), 1) // nkv).astype(BF16)
        fnew = _dot_sel(uen, lfn_ref[0], sel_first=False, dims=TN)[:nh] + f_sc[...][:, 0:1]
        rq, cq, own_n = own_kv(nn)
        keep = own_n & ((cq // nkv) <= (rq // nh)) & (cq < ts * nkv)
        s = lax.dot_general(q, kn_ref[0], NT, preferred_element_type=F32) * scale - jnp.concatenate([fnew] * ts, axis=0)
        update(jnp.where(keep, s, NEG_INF), lambda pr: jnp.dot(pr, vn_ref[0], preferred_element_type=F32))
        o_ref[0] = (acc_sc[...] / l_sc[...]).astype(o_ref.dtype)


def _fox_sample(page_ids, qb, knew, vnew, lfnew, kc, vc, lfc, nkv, g, ts):
    bs, rows, _ = qb.shape
    npages = page_ids.shape[1]
    npg = _pick(npages, 16, 1)
    page = kc.shape[1]
    nh = lfc.shape[2]
    nn = knew.shape[1]
    nl = lfnew.shape[1]

    def kv_page(i):
        return pl.BlockSpec((1, page, nkv, LANE), lambda b, p, pt, i=i: (pt[b * npages + p * npg + i], 0, 0, 0))

    def lf_page(i):
        return pl.BlockSpec((1, page, nh), lambda b, p, pt, i=i: (pt[b * npages + p * npg + i], 0, 0))

    per_b = lambda r: pl.BlockSpec((1, r, LANE), lambda b, p, pt: (b, 0, 0))
    in_specs = [per_b(rows), per_b(nn), per_b(nn), per_b(nl)]
    in_specs += [kv_page(i) for i in range(npg)] + [kv_page(i) for i in range(npg)] + [lf_page(i) for i in range(npg)]
    grid_spec = pltpu.PrefetchScalarGridSpec(
        num_scalar_prefetch=1,
        grid=(bs, npages // npg),
        in_specs=in_specs,
        out_specs=per_b(rows),
        scratch_shapes=[pltpu.VMEM((rows, 1), F32), pltpu.VMEM((rows, 1), F32), pltpu.VMEM((rows, LANE), F32),
                        pltpu.VMEM((nh, LANE), F32)],
    )
    return pl.pallas_call(
        functools.partial(_fox_sample_kernel, npg=npg, nkv=nkv, g=g, ts=ts, scale=LANE ** -0.5),
        grid_spec=grid_spec,
        out_shape=jax.ShapeDtypeStruct((bs, rows, LANE), BF16),
        compiler_params=_params("parallel", "arbitrary"),
        name="fox_sample",
    )(page_ids.reshape(-1), qb, knew, vnew, lfnew, *([kc] * npg), *([vc] * npg), *([lfc] * npg))


def _gdn_kernel(*refs, hb, c, taps, use_state, scale):
    if use_state:
        (q_ref, k_ref, v_ref, wq_ref, wk_ref, wv_ref, z_ref, gb_ref, nw_ref, hq_ref, hk_ref, hv_ref, s0_ref,
         o_ref, so_ref, s_sc, tail_sc) = refs
    else:
        q_ref, k_ref, v_ref, wq_ref, wk_ref, wv_ref, z_ref, gb_ref, nw_ref, o_ref, so_ref, s_sc, tail_sc = refs
    n = pl.program_id(2)

    @pl.when(n == 0)
    def _():
        s_sc[...] = s0_ref[0] if use_state else jnp.zeros(s_sc.shape, F32)
        if use_state:
            for i, h_ref in enumerate((hq_ref, hk_ref, hv_ref)):
                tail_sc[i] = h_ref[...]
        else:
            tail_sc[...] = jnp.zeros(tail_sc.shape, F32)

    heads = range(hb)

    def conv_act(x_ref, w_ref, idx, l2_scale):
        x = x_ref[...]
        ext = jnp.concatenate([tail_sc[idx], x], axis=0)
        tail_sc[idx] = x[c - SUBLANE:c]
        y = None
        for tap in range(taps):
            off = SUBLANE - (taps - 1 - tap)
            term = ext[off:off + c] * w_ref[tap:tap + 1, :]
            y = term if y is None else y + term
        y = _silu(y)
        cols = [y[:, h * LANE:(h + 1) * LANE] for h in heads]
        if l2_scale is not None:
            cols = [yh * (lax.rsqrt(jnp.sum(yh * yh, axis=-1, keepdims=True) + L2_EPS) * l2_scale) for yh in cols]
        return jnp.stack(cols)

    q = conv_act(q_ref, wq_ref, 0, scale)
    k = conv_act(k_ref, wk_ref, 1, 1.0)
    v = conv_act(v_ref, wv_ref, 2, None)
    gb = gb_ref[0]
    r = lax.broadcasted_iota(jnp.int32, (c, c), 0)
    cc = lax.broadcasted_iota(jnp.int32, (c, c), 1)
    incl, strict = r >= cc, r > cc
    gc_all = _dot_sel(incl.astype(BF16), gb)
    gc_t = gc_all.T
    gcol = jnp.stack([gc_all[:, h:h + 1] for h in heads])
    grow = jnp.stack([gc_t[h:h + 1, :] for h in heads])
    beta = jnp.stack([gb[:, hb + h:hb + h + 1] for h in heads])
    glast = gcol[:, c - 1:c, :]
    decay = jnp.exp(jnp.where(incl[None], gcol - grow, NEG_INF))
    kb = k * beta
    bdot = lambda a, b, eq: jnp.einsum(eq, a.astype(BF16), b.astype(BF16), preferred_element_type=F32)

    mm = jnp.where(strict[None], bdot(kb, k, 'hck,hsk->hcs') * decay, 0.0)
    eye = (r == cc).astype(F32)[None]
    base = min(c, SUBLANE)
    same = lambda s: ((r // s) == (cc // s))[None]
    mp = jnp.where(same(base), mm, 0.0)
    tinv = eye - mp
    span = 2
    while span < base:
        mp = bdot(mp, mp, 'hcs,hsk->hck')
        tinv = bdot(tinv, eye + mp, 'hcs,hsk->hck')
        span *= 2
    blk = base
    while blk < c:
        off = jnp.where(same(2 * blk) & ~same(blk), mm, 0.0)
        tinv = tinv - bdot(tinv, bdot(off, tinv, 'hcs,hsk->hck'), 'hcs,hsk->hck')
        blk *= 2
    egc = jnp.exp(gcol)
    u = bdot(tinv, v * beta, 'hcs,hsv->hcv')
    w = bdot(tinv, kb * egc, 'hcs,hsk->hck')
    a = bdot(q, k, 'hck,hsk->hcs') * decay
    qd = q * egc
    kd = k * jnp.exp(glast - gcol)
    s = s_sc[...]
    v_new = u - bdot(w, s, 'hck,hkv->hcv')
    o = bdot(qd, s, 'hck,hkv->hcv') + bdot(a, v_new, 'hcs,hsv->hcv')
    s_new = s * jnp.exp(glast) + bdot(kd, v_new, 'hck,hcv->hkv')
    s_sc[...] = s_new

    @pl.when(n == pl.num_programs(2) - 1)
    def _():
        so_ref[0] = s_new

    on = o * lax.rsqrt(jnp.mean(o * o, axis=-1, keepdims=True) + RMS_EPS) * nw_ref[...][None]
    for h in heads:
        sl = slice(h * LANE, (h + 1) * LANE)
        o_ref[:, sl] = (on[h] * _silu(z_ref[:, sl])).astype(o_ref.dtype)


def _gdn(x, conv_w, gb, norm_w, hist, s0, nb, tseq, c, hb, nheads):
    m = x.shape[0]
    n = tseq // c
    w = hb * LANE
    ng = nheads // hb
    kw = nheads * LANE
    taps = conv_w.shape[0]
    row_col = lambda cb: pl.BlockSpec((c, w), lambda b, hg, i, cb=cb: (b * n + i, cb + hg))
    wcol = lambda cb: pl.BlockSpec((taps, w), lambda b, hg, i, cb=cb: (0, cb + hg))
    in_specs = [row_col(0), row_col(kw // w), row_col(2 * kw // w),
                wcol(0), wcol(kw // w), wcol(2 * kw // w),
                row_col(3 * kw // w),
                pl.BlockSpec((1, c, LANE), lambda b, hg, i: (hg, b * n + i, 0)),
                pl.BlockSpec((1, LANE), lambda b, hg, i: (0, 0))]
    args = [x, x, x, conv_w, conv_w, conv_w, x, gb, norm_w.reshape(1, LANE)]
    state_spec = pl.BlockSpec((1, hb, LANE, LANE), lambda b, hg, i: (b, hg, 0, 0))
    if s0 is not None:
        hcol = lambda cb: pl.BlockSpec((SUBLANE, w), lambda b, hg, i, cb=cb: (b, cb + hg))
        in_specs += [hcol(0), hcol(kw // w), hcol(2 * kw // w), state_spec]
        args += [hist, hist, hist, s0]
    return pl.pallas_call(
        functools.partial(_gdn_kernel, hb=hb, c=c, taps=taps, use_state=s0 is not None, scale=LANE ** -0.5),
        grid=(nb, ng, n),
        in_specs=in_specs,
        out_specs=[pl.BlockSpec((c, w), lambda b, hg, i: (b * n + i, hg)), state_spec],
        out_shape=[jax.ShapeDtypeStruct((m, kw), BF16), jax.ShapeDtypeStruct((nb, nheads, LANE, LANE), F32)],
        scratch_shapes=[pltpu.VMEM((hb, LANE, LANE), F32), pltpu.VMEM((3, SUBLANE, w), F32)],
        compiler_params=_params("parallel", "parallel", "arbitrary"),
        name="gdn",
    )(*args)


def _merge_kernel(of_ref, og_ref, wf_ref, wg_ref, gf_ref, gg_ref, o_ref):
    a = jnp.dot(of_ref[...], wf_ref[...].astype(BF16), preferred_element_type=F32)
    b = jnp.dot(og_ref[...], wg_ref[...].astype(BF16), preferred_element_type=F32)
    o_ref[...] = (jax.nn.sigmoid(gf_ref[...]) * a + jax.nn.sigmoid(gg_ref[...]) * b).astype(o_ref.dtype)


def _merge(o_fox, o_gdn, w_bf, w_bg, proj, gf_col0, gg_col0, tm, tn):
    m, kf = o_fox.shape
    kg = o_gdn.shape[1]
    n = w_bf.shape[1]
    return pl.pallas_call(
        _merge_kernel,
        grid=(m // tm, n // tn),
        in_specs=[pl.BlockSpec((tm, kf), lambda i, j: (i, 0)),
                  pl.BlockSpec((tm, kg), lambda i, j: (i, 0)),
                  pl.BlockSpec((kf, tn), lambda i, j: (0, j)),
                  pl.BlockSpec((kg, tn), lambda i, j: (0, j)),
                  pl.BlockSpec((tm, tn), lambda i, j: (i, gf_col0 // tn + j)),
                  pl.BlockSpec((tm, tn), lambda i, j: (i, gg_col0 // tn + j))],
        out_specs=pl.BlockSpec((tm, tn), lambda i, j: (i, j)),
        out_shape=jax.ShapeDtypeStruct((m, n), BF16),
        compiler_params=_params("parallel", "arbitrary"),
        name="merge",
    )(o_fox, o_gdn, w_bf, w_bg, proj, proj)


def _resid_kernel(a_ref, w_ref, x_ref, gt_ref, o_ref):
    acc = jnp.dot(a_ref[...], w_ref[...].astype(BF16), preferred_element_type=F32)
    o_ref[0] = x_ref[0] + _rows(acc, gt_ref[0]) * acc


def _resid_matmul(a, w, x3, gt3, tiles_per_mod, tn, name):
    nt, tm, n = x3.shape
    k = a.shape[1]
    mr = gt3.shape[1]
    return pl.pallas_call(
        _resid_kernel,
        grid=(nt, n // tn),
        in_specs=[pl.BlockSpec((tm, k), lambda i, j: (i, 0)),
                  pl.BlockSpec((k, tn), lambda i, j: (0, j)),
                  pl.BlockSpec((1, tm, tn), lambda i, j: (i, 0, j)),
                  pl.BlockSpec((1, mr, tn), lambda i, j: (i // tiles_per_mod, 0, j))],
        out_specs=pl.BlockSpec((1, tm, tn), lambda i, j: (i, 0, j)),
        out_shape=jax.ShapeDtypeStruct((nt, tm, n), F32),
        compiler_params=_params("parallel", "arbitrary"),
        name=name,
    )(a, w, x3, gt3)


def _ffn_up_kernel(*refs, stride, taps, tiles_per_seq, halo_from_h):
    if halo_from_h:
        hh_ref, h_ref, wg_ref, wu_ref, cw_ref, act_ref, tail_ref, hcat = refs
    else:
        hist_ref, h_ref, wg_ref, wu_ref, cw_ref, act_ref, tail_ref = refs
    i, j = pl.program_id(0), pl.program_id(1)
    tm = h_ref.shape[0]
    if halo_from_h:
        hh = hh_ref.shape[0]

        @pl.when(j == 0)
        def _():
            hcat[0:hh, :] = jnp.where(i % tiles_per_seq == 0, jnp.zeros_like(hh_ref[...]), hh_ref[...])
            hcat[hh:hh + tm, :] = h_ref[...]

        ext = jnp.dot(hcat[...], wg_ref[...].astype(BF16), preferred_element_type=F32)
    else:
        hh = hist_ref.shape[0]
        ext = jnp.concatenate(
            [hist_ref[...], jnp.dot(h_ref[...], wg_ref[...].astype(BF16), preferred_element_type=F32)], axis=0)
    up = jnp.dot(h_ref[...], wu_ref[...].astype(BF16), preferred_element_type=F32)
    y = None
    for tap in range(taps):
        off = hh - (taps - 1 - tap) * stride
        term = ext[off:off + tm] * cw_ref[tap:tap + 1, :]
        y = term if y is None else y + term
    act_ref[...] = (_silu(y) * up).astype(act_ref.dtype)
    tail_ref[0] = ext[hh + tm - tail_ref.shape[1]:hh + tm]


def _ffn_up(h, hist, w_gate, w_up, conv_w, tm, tn, stride, tiles_per_seq, tail_rows):
    m, k = h.shape
    n = w_gate.shape[1]
    taps = conv_w.shape[0]
    halo_from_h = hist is None
    if halo_from_h:
        hh = BF16_ROWS
        first = pl.BlockSpec((hh, k), lambda i, j: (jnp.maximum(i * (tm // hh) - 1, 0), 0))
        first_arg = h
        scratch = [pltpu.VMEM((hh + tm, k), BF16)]
    else:
        hh = hist.shape[0]
        first = pl.BlockSpec((hh, tn), lambda i, j: (0, j))
        first_arg = hist
        scratch = []
    return pl.pallas_call(
        functools.partial(_ffn_up_kernel, stride=stride, taps=taps, tiles_per_seq=tiles_per_seq,
                          halo_from_h=halo_from_h),
        grid=(m // tm, n // tn),
        in_specs=[first,
                  pl.BlockSpec((tm, k), lambda i, j: (i, 0)),
                  pl.BlockSpec((k, tn), lambda i, j: (0, j)),
                  pl.BlockSpec((k, tn), lambda i, j: (0, j)),
                  pl.BlockSpec((taps, tn), lambda i, j: (0, j))],
        out_specs=[pl.BlockSpec((tm, tn), lambda i, j: (i, j)),
                   pl.BlockSpec((1, tail_rows, tn), lambda i, j: (i, 0, j))],
        out_shape=[jax.ShapeDtypeStruct((m, n), BF16), jax.ShapeDtypeStruct((m // tm, tail_rows, n), F32)],
        scratch_shapes=scratch,
        compiler_params=_params("parallel", "arbitrary"),
        name="ffn_up",
    )(first_arg, h, w_gate, w_up, conv_w)


def _layer(x2, mods, wts, lay, *, prompt, nb, tseq, fox_fn, gdn_s0, gdn_hist, ffn_hist):
    m, d = x2.shape
    sh_m, sc_m, gt_m, sh_f, sc_f, gt_f = mods
    nh, nkv, g, gh = lay["nh"], lay["nkv"], lay["g"], lay["gh"]
    kw = gh * LANE
    if prompt:
        tr = _pick(tseq, 256, SUBLANE)
        tm = _pick(tseq, 1024, LANE)
        stride = 1
    else:
        tr = nb
        tm = m
        stride = nb
    tiles_mod = (tseq // tr) if prompt else m // tr
    h = _normmod(x2.reshape(m // tr, tr, d), wts["norm_mix"], sc_m, sh_m, tiles_mod).reshape(m, d)
    mm = lambda wt: _matmul(h, wt, tm, _pick(wt.shape[0], 512, LANE))
    proj_fox, proj_gdn, proj_gate = mm(wts["w_fox"]), mm(wts["w_gdn"]), mm(wts["w_gate"])
    small = _small_heads(h, wts["w_small"], wts["p_small"], nh, tm)
    qn, kn, vb, vt, k_f, v_f = _qknorm(proj_fox, wts["fox_q_norm"], wts["fox_k_norm"], nh, nkv, _pick(m, 256, LANE))
    o_fox = fox_fn(qn, kn, vb, vt, small)
    hb = lay["hb"]
    if prompt:
        o_gdn, s_fin = _gdn(proj_gdn, wts["gdn_conv_w"], _group_gb(small, nh, gh, hb), wts["gdn_norm"], None, None,
                            nb, tseq, lay["chunk"], hb, gh)
    else:
        ts = m // nb
        cpad = lay["cpad"]
        to_bm = lambda a: jnp.pad(a.reshape(ts, nb, -1).transpose(1, 0, 2),
                                  ((0, 0), (0, cpad - ts), (0, 0))).reshape(nb * cpad, -1)
        o_bm, s_fin = _gdn(to_bm(proj_gdn), wts["gdn_conv_w"], _group_gb(to_bm(small), nh, gh, hb), wts["gdn_norm"],
                           gdn_hist, gdn_s0, nb, cpad, cpad, hb, gh)
        o_gdn = o_bm.reshape(nb, cpad, kw)[:, :ts].transpose(1, 0, 2).reshape(m, kw)
    tn = _pick(d, 512, LANE)
    merged = _merge(o_fox, o_gdn, wts["w_branch_fox"], wts["w_branch_gdn"], proj_gate, 0, d, tm, tn)
    tiles_mod_m = (tseq // tm) if prompt else 1
    x3 = _resid_matmul(merged, wts["w_out"], x2.reshape(m // tm, tm, d), gt_m, tiles_mod_m, tn, "out_proj")
    x2 = x3.reshape(m, d)
    h2 = _normmod(x2.reshape(m // tr, tr, d), wts["norm_ffn"], sc_f, sh_f, tiles_mod).reshape(m, d)
    dff = wts["ffn_w_gate"].shape[1]
    tnf = _pick(dff, 256, LANE)
    act, tail = _ffn_up(h2, ffn_hist, wts["ffn_w_gate"], wts["ffn_w_up"], wts["ffn_conv_w"], tm, tnf, stride,
                        tseq // tm if prompt else 1, SUBLANE if prompt else m)
    tm2 = _pick(tm, 512, LANE) if prompt else m
    y3 = _resid_matmul(act, wts["ffn_w_down"], x2.reshape(m // tm2, tm2, d), gt_f,
                       (tseq // tm2) if prompt else 1, _pick(d, 256, LANE), "ffn_down")
    return y3.reshape(m, d), dict(gdn=proj_gdn, small=small, k=k_f, v=v_f, s=s_fin, tail=tail)


def _group_gb(small, nh, gh, hb):
    rows = small.shape[0]
    gg = small[:, nh:nh + gh].reshape(rows, gh // hb, hb)
    bb = small[:, nh + gh:nh + 2 * gh].reshape(rows, gh // hb, hb)
    gb = jnp.concatenate([gg, bb], axis=-1).transpose(1, 0, 2)
    return jnp.pad(gb, ((0, 0), (0, 0), (0, LANE - 2 * hb)))


def kernel(x_prompt, x_sample, cache_k, cache_v, cache_logf, state_gdn, state_gdn_conv, state_ffn_conv, page_table, c_prompt, c_sample, w_ada, b_ada, norm_mix, norm_ffn, w_in, fox_b_f, fox_q_norm, fox_k_norm, gdn_conv_w, gdn_A_log, gdn_dt_bias, gdn_norm, w_branch_fox, w_branch_gdn, w_out, ffn_w_gate, ffn_w_up, ffn_conv_w, ffn_w_down):
    nb, t, d = x_prompt.shape
    bs, ts, _ = x_sample.shape
    depth = w_in.shape[0]
    page, nkv = cache_k.shape[2], cache_k.shape[3]
    nh = cache_logf.shape[-1]
    g = nh // nkv
    gh = state_gdn.shape[2]
    kw = gh * LANE
    gconv = gdn_conv_w.shape[1]
    fconv = ffn_conv_w.shape[1]
    dff = ffn_w_gate.shape[-1]
    assert cache_k.shape[-1] == LANE and state_gdn.shape[-1] == LANE and state_gdn.shape[-2] == LANE
    assert gh == nh and ts >= gconv - 1 and ts >= fconv - 1

    sizes = (nh * LANE, nkv * LANE, nkv * LANE, nh, 3 * kw, kw, gh, gh, d, d)
    starts = [0]
    for s_ in sizes:
        starts.append(starts[-1] + s_)
    assert starts[-1] == w_in.shape[-1]
    hb = _pick(gh, 16, 1)
    lay = dict(nh=nh, nkv=nkv, g=g, gh=gh, hb=hb, chunk=_pick(t, 64, SUBLANE), cpad=SUBLANE)

    xs = x_sample.transpose(1, 0, 2).reshape(ts * bs, d)
    xp = x_prompt.reshape(nb * t, d)
    mc = -(-nb // SUBLANE) * SUBLANE
    c_all = jnp.concatenate([c_prompt, jnp.zeros((mc - nb, d), F32), c_sample], axis=0)

    outs_p = [[] for _ in range(6)]
    outs_s = [[] for _ in range(6)]
    for l in range(depth):
        nin = w_in.shape[2]
        wt = jnp.swapaxes(w_in, 1, 2).reshape(depth * nin, w_in.shape[1])
        run = lambda a, b: _wprep(wt, l * nin + starts[a], starts[b] - starts[a])
        w_small = _wsmall(wt, l * nin + starts[3], l * nin + starts[6], nh)
        zpad = jnp.zeros((LANE - 2 * nh,), F32)
        p_small = jnp.stack([jnp.concatenate([fox_b_f[l], gdn_dt_bias[l], zpad]),
                             jnp.concatenate([jnp.zeros((nh,), F32), gdn_A_log[l], zpad])])
        p_small = jnp.pad(p_small, ((0, SUBLANE - 2), (0, 0)))
        wts = dict(norm_mix=norm_mix[l], norm_ffn=norm_ffn[l], w_fox=run(0, 3), w_gdn=run(4, 6), w_gate=run(8, 10),
                   w_small=w_small, p_small=p_small,
                   fox_q_norm=fox_q_norm[l], fox_k_norm=fox_k_norm[l], gdn_conv_w=gdn_conv_w[l],
                   gdn_norm=gdn_norm[l], w_branch_fox=w_branch_fox[l], w_branch_gdn=w_branch_gdn[l],
                   w_out=w_out[l], ffn_w_gate=ffn_w_gate[l], ffn_w_up=ffn_w_up[l],
                   ffn_conv_w=ffn_conv_w[l], ffn_w_down=ffn_w_down[l].astype(BF16))

        mod = _ada(c_all, w_ada[l], b_ada[l])
        mods_p = [mod[:nb, i * d:(i + 1) * d].reshape(nb, 1, d) for i in range(6)]
        mods_s = [mod[mc:, i * d:(i + 1) * d].reshape(1, bs, d) for i in range(6)]

        def fox_p(qn, kn, vb, vt, small):
            f_tm = _cumf(small, nb, t)
            fcol = f_tm[:, :nh].reshape(nb, t, nkv, g).transpose(0, 2, 1, 3)
            fcol = jnp.pad(fcol, ((0, 0), (0, 0), (0, 0), (0, LANE - g)))
            return _fox_prompt(qn, kn, vt, fcol, nb, t, nkv, g)

        xp, st = _layer(xp, mods_p, wts, lay, prompt=True, nb=nb, tseq=t, fox_fn=fox_p, gdn_s0=None,
                        gdn_hist=None, ffn_hist=None)
        outs_p[0].append(st["k"].reshape(nb, t, nkv, LANE))
        outs_p[1].append(st["v"].reshape(nb, t, nkv, LANE))
        outs_p[2].append(st["small"].reshape(nb, t, LANE)[:, :, :nh])
        outs_p[3].append(st["s"])
        outs_p[4].append(st["gdn"].reshape(nb, t, -1)[:, t - (gconv - 1):, :3 * kw])
        tail = st["tail"].reshape(nb, -1, SUBLANE, dff)
        outs_p[5].append(tail[:, -1, SUBLANE - (fconv - 1):, :])

        n_pool = cache_k.shape[1]
        kc = cache_k.reshape(depth * n_pool, page, nkv, LANE)
        vc = cache_v.reshape(depth * n_pool, page, nkv, LANE)
        lfc = cache_logf.reshape(depth * n_pool, page, nh)
        page_ids = page_table + l * n_pool

        def fox_s(qn, kn, vb, vt, small):
            bm = lambda a, n: a.reshape(ts, bs, n, LANE).transpose(1, 0, 2, 3).reshape(bs, ts * n, LANE)
            nn = -(-ts * nkv // BF16_ROWS) * BF16_ROWS
            padr = lambda a, n: jnp.pad(a, ((0, 0), (0, n - a.shape[1]), (0, 0)))
            lfn = padr(small.reshape(ts, bs, LANE).transpose(1, 0, 2), BF16_ROWS)
            o = _fox_sample(page_ids, bm(qn, nh), padr(bm(kn, nkv), nn), padr(bm(vb, nkv), nn), lfn, kc, vc, lfc,
                            nkv, g, ts)
            return o.reshape(bs, ts, nh, LANE).transpose(1, 0, 2, 3).reshape(ts * bs, nh * LANE)

        ghist = jnp.pad(state_gdn_conv[l], ((0, 0), (SUBLANE - (gconv - 1), 0), (0, 0))).reshape(bs * SUBLANE, 3 * kw)
        fhist = state_ffn_conv[l].transpose(1, 0, 2).reshape((fconv - 1) * bs, dff)
        xs, st = _layer(xs, mods_s, wts, lay, prompt=False, nb=bs, tseq=ts, fox_fn=fox_s, gdn_s0=state_gdn[l],
                        gdn_hist=ghist, ffn_hist=fhist)
        bm3 = lambda a: a.reshape(ts, bs, -1).transpose(1, 0, 2)
        outs_s[0].append(bm3(st["k"]).reshape(bs, ts, nkv, LANE))
        outs_s[1].append(bm3(st["v"]).reshape(bs, ts, nkv, LANE))
        outs_s[2].append(bm3(st["small"])[:, :, :nh])
        outs_s[3].append(st["s"])
        graw = bm3(st["gdn"])[:, :, :3 * kw]
        outs_s[4].append(jnp.concatenate([state_gdn_conv[l], graw], axis=1)[:, -(gconv - 1):])
        fraw = bm3(st["tail"].reshape(ts * bs, dff))
        outs_s[5].append(jnp.concatenate([state_ffn_conv[l], fraw], axis=1)[:, -(fconv - 1):])

    y_prompt = xp.reshape(nb, t, d)
    y_sample = xs.reshape(ts, bs, d).transpose(1, 0, 2)
    return (y_prompt, y_sample, *(jnp.stack(a) for a in outs_p), *(jnp.stack(a) for a in outs_s))
```

```python
import functools

import jax
import jax.numpy as jnp
from jax import lax
from jax.experimental import pallas as pl
from jax.experimental.pallas import tpu as pltpu

F32 = jnp.float32
BF16 = jnp.bfloat16
RMS_EPS = 1e-6
L2_EPS = 1e-6
NEG_INF = -1e30
LANE = 128
SUBLANE = 8
BF16_ROWS = 16
VMEM_LIMIT_BYTES = 56 * 2 ** 20

NN = (((1,), (0,)), ((), ()))
NT = (((1,), (1,)), ((), ()))
TN = (((0,), (0,)), ((), ()))


def _pick(n, pref, align):
    t = min(pref, n) // align * align
    while t >= align:
        if n % t == 0:
            return t
        t -= align
    return n


def _params(*sem):
    return pltpu.CompilerParams(dimension_semantics=sem, vmem_limit_bytes=VMEM_LIMIT_BYTES)


def _dot(a, b, dims=NN):
    return lax.dot_general(a.astype(BF16), b.astype(BF16), dims, preferred_element_type=F32)


def _dot_sel(sel, x, sel_first=True, dims=NN):
    h = x.astype(BF16)
    r = x - h.astype(F32)
    m = r.astype(BF16)
    l = (r - m.astype(F32)).astype(BF16)
    if sel_first:
        d = lambda p: lax.dot_general(sel, p, dims, preferred_element_type=F32)
    else:
        d = lambda p: lax.dot_general(p, sel, dims, preferred_element_type=F32)
    return d(h) + (d(m) + d(l))


def _silu(x):
    return x * jax.nn.sigmoid(x)


def _softplus(x):
    return jnp.maximum(x, 0.0) + jnp.log1p(jnp.exp(-jnp.abs(x)))


def _rows(a, m):
    if m.shape[0] == 1 or m.shape[0] == a.shape[0]:
        return m
    reps = a.shape[0] // m.shape[0]
    return jnp.concatenate([m] * reps, axis=0)


def _ada_kernel(c_ref, w_ref, b_ref, o_ref):
    a = _silu(c_ref[...])
    o_ref[...] = _dot(a, w_ref[...]) + b_ref[...]


def _ada(c_all, w_ada, b_ada):
    mc, d = c_all.shape
    n = w_ada.shape[1]
    tn = _pick(n, 512, LANE)
    return pl.pallas_call(
        _ada_kernel,
        grid=(n // tn,),
        in_specs=[pl.BlockSpec((mc, d), lambda j: (0, 0)),
                  pl.BlockSpec((d, tn), lambda j: (0, j)),
                  pl.BlockSpec((1, tn), lambda j: (0, j))],
        out_specs=pl.BlockSpec((mc, tn), lambda j: (0, j)),
        out_shape=jax.ShapeDtypeStruct((mc, n), F32),
        compiler_params=_params("parallel"),
        name="ada",
    )(c_all, w_ada, b_ada.reshape(1, n))


def _normmod_kernel(x_ref, g_ref, sc_ref, sh_ref, o_ref):
    x = x_ref[0]
    y = x * lax.rsqrt(jnp.mean(x * x, axis=-1, keepdims=True) + RMS_EPS) * g_ref[...]
    o_ref[0] = (y * (1.0 + _rows(y, sc_ref[0])) + _rows(y, sh_ref[0])).astype(o_ref.dtype)


def _normmod(x3, gain, sc3, sh3, tiles_per_mod):
    nt, tr, d = x3.shape
    mr = sc3.shape[1]
    mod_spec = pl.BlockSpec((1, mr, d), lambda i: (i // tiles_per_mod, 0, 0))
    return pl.pallas_call(
        _normmod_kernel,
        grid=(nt,),
        in_specs=[pl.BlockSpec((1, tr, d), lambda i: (i, 0, 0)),
                  pl.BlockSpec((1, d), lambda i: (0, 0)),
                  mod_spec, mod_spec],
        out_specs=pl.BlockSpec((1, tr, d), lambda i: (i, 0, 0)),
        out_shape=jax.ShapeDtypeStruct((nt, tr, d), BF16),
        compiler_params=_params("parallel"),
        name="normmod",
    )(x3, gain.reshape(1, d), sc3, sh3)


def _wprep_kernel(a_ref, b_ref, o_ref, *, delta):
    a = a_ref[...]
    if delta:
        a = jnp.concatenate([a[delta:], b_ref[0:delta, :]], axis=0)
    o_ref[...] = a.astype(BF16)


def _wprep(wt, row0, rows):
    k = wt.shape[1]
    tr = _pick(rows, 512, LANE)
    tk = _pick(k, 2048, LANE)
    delta, rb = row0 % tr, row0 // tr
    assert delta < LANE
    nxt = (lambda i, j: ((rb + i + 1) * (tr // LANE), j)) if delta else (lambda i, j: (0, j))
    return pl.pallas_call(
        functools.partial(_wprep_kernel, delta=delta),
        grid=(rows // tr, k // tk),
        in_specs=[pl.BlockSpec((tr, tk), lambda i, j: (rb + i, j)), pl.BlockSpec((LANE, tk), nxt)],
        out_specs=pl.BlockSpec((tr, tk), lambda i, j: (i, j)),
        out_shape=jax.ShapeDtypeStruct((rows, k), BF16),
        compiler_params=_params("parallel", "parallel"),
        name="wprep",
    )(wt, wt)


def _wsmall_kernel(a_ref, b_ref, o_ref, *, nh):
    row = lax.broadcasted_iota(jnp.int32, a_ref.shape, 0)
    o_ref[...] = jnp.where(row < nh, a_ref[...], jnp.where(row < 3 * nh, b_ref[...], 0.0)).astype(BF16)


def _wsmall(wt, row_f, row_ab, nh):
    assert row_f % LANE == 0 and row_ab % LANE == nh and 3 * nh <= LANE
    k = wt.shape[1]
    tk = _pick(k, 2048, LANE)
    return pl.pallas_call(
        functools.partial(_wsmall_kernel, nh=nh),
        grid=(k // tk,),
        in_specs=[pl.BlockSpec((LANE, tk), lambda j: (row_f // LANE, j)),
                  pl.BlockSpec((LANE, tk), lambda j: (row_ab // LANE, j))],
        out_specs=pl.BlockSpec((LANE, tk), lambda j: (0, j)),
        out_shape=jax.ShapeDtypeStruct((LANE, k), BF16),
        compiler_params=_params("parallel"),
        name="wsmall",
    )(wt, wt)


def _mm_kernel(x_ref, w_ref, o_ref):
    o_ref[...] = lax.dot_general(x_ref[...], w_ref[...], NT, preferred_element_type=F32).astype(o_ref.dtype)


def _matmul(x, wt, tm, tn, out_dtype=F32):
    m, k = x.shape
    n = wt.shape[0]
    return pl.pallas_call(
        _mm_kernel,
        grid=(m // tm, n // tn),
        in_specs=[pl.BlockSpec((tm, k), lambda i, j: (i, 0)),
                  pl.BlockSpec((tn, k), lambda i, j: (j, 0))],
        out_specs=pl.BlockSpec((tm, tn), lambda i, j: (i, j)),
        out_shape=jax.ShapeDtypeStruct((m, n), out_dtype),
        compiler_params=_params("parallel", "arbitrary"),
        name="inproj",
    )(x, wt)


def _small_kernel(x_ref, w_ref, p_ref, o_ref, *, nh):
    acc = lax.dot_general(x_ref[...], w_ref[...], NT, preferred_element_type=F32)
    xb = acc + p_ref[0:1, :]
    lane = lax.broadcasted_iota(jnp.int32, acc.shape, 1)
    sp = _softplus(xb)
    logf = jnp.minimum(xb, 0.0) - jnp.log1p(jnp.exp(-jnp.abs(xb)))
    g = -jnp.exp(p_ref[1:2, :]) * sp
    beta = jax.nn.sigmoid(acc)
    o_ref[...] = jnp.where(lane < nh, logf, jnp.where(lane < 2 * nh, g, jnp.where(lane < 3 * nh, beta, 0.0)))


def _small_heads(h, w_small, p_small, nh, tm):
    m, k = h.shape
    return pl.pallas_call(
        functools.partial(_small_kernel, nh=nh),
        grid=(m // tm,),
        in_specs=[pl.BlockSpec((tm, k), lambda i: (i, 0)),
                  pl.BlockSpec((LANE, k), lambda i: (0, 0)),
                  pl.BlockSpec((SUBLANE, LANE), lambda i: (0, 0))],
        out_specs=pl.BlockSpec((tm, LANE), lambda i: (i, 0)),
        out_shape=jax.ShapeDtypeStruct((m, LANE), F32),
        compiler_params=_params("parallel"),
        name="small_heads",
    )(h, w_small, p_small)


def _qknorm_kernel(x_ref, qg_ref, kg_ref, qn_ref, kn_ref, vb_ref, vt_ref, kf_ref, vf_ref, *, nq, nkv):
    def norm(x, g):
        return x * lax.rsqrt(jnp.mean(x * x, axis=-1, keepdims=True) + RMS_EPS) * g

    for h in range(nq):
        sl = slice(h * LANE, (h + 1) * LANE)
        qn_ref[:, sl] = norm(x_ref[:, sl], qg_ref[...]).astype(BF16)
    for h in range(nkv):
        sl = slice(h * LANE, (h + 1) * LANE)
        kx = norm(x_ref[:, (nq + h) * LANE:(nq + h + 1) * LANE], kg_ref[...])
        kf_ref[:, sl] = kx
        kn_ref[:, sl] = kx.astype(BF16)
        vx = x_ref[:, (nq + nkv + h) * LANE:(nq + nkv + h + 1) * LANE]
        vf_ref[:, sl] = vx
        vb_ref[:, sl] = vx.astype(BF16)
        vt_ref[sl, :] = vx.T.astype(BF16)


def _qknorm(proj, q_gain, k_gain, nq, nkv, tm):
    m = proj.shape[0]
    wq, wk = nq * LANE, nkv * LANE
    row = lambda w: pl.BlockSpec((tm, w), lambda i: (i, 0))
    gain = pl.BlockSpec((1, LANE), lambda i: (0, 0))
    return pl.pallas_call(
        functools.partial(_qknorm_kernel, nq=nq, nkv=nkv),
        grid=(m // tm,),
        in_specs=[row(wq + 2 * wk), gain, gain],
        out_specs=[row(wq), row(wk), row(wk), pl.BlockSpec((wk, tm), lambda i: (0, i)), row(wk), row(wk)],
        out_shape=[jax.ShapeDtypeStruct((m, wq), BF16), jax.ShapeDtypeStruct((m, wk), BF16),
                   jax.ShapeDtypeStruct((m, wk), BF16), jax.ShapeDtypeStruct((wk, m), BF16),
                   jax.ShapeDtypeStruct((m, wk), F32), jax.ShapeDtypeStruct((m, wk), F32)],
        compiler_params=_params("parallel"),
        name="qknorm",
    )(proj, q_gain.reshape(1, LANE), k_gain.reshape(1, LANE))


def _cumf_kernel(x_ref, o_ref, *, blk):
    t = x_ref.shape[0]
    r = lax.broadcasted_iota(jnp.int32, (blk, blk), 0)
    c = lax.broadcasted_iota(jnp.int32, (blk, blk), 1)
    tri = (r >= c).astype(BF16)
    carry = jnp.zeros((1, LANE), F32)
    for i in range(t // blk):
        cum = _dot_sel(tri, x_ref[i * blk:(i + 1) * blk, :]) + carry
        carry = cum[blk - 1:blk, :]
        o_ref[i * blk:(i + 1) * blk, :] = cum


def _cumf(small, nb, t):
    blk = _pick(t, 256, LANE)
    return pl.pallas_call(
        functools.partial(_cumf_kernel, blk=blk),
        grid=(nb,),
        in_specs=[pl.BlockSpec((t, LANE), lambda b: (b, 0))],
        out_specs=pl.BlockSpec((t, LANE), lambda b: (b, 0)),
        out_shape=jax.ShapeDtypeStruct((nb * t, LANE), F32),
        compiler_params=_params("parallel"),
        name="cumf",
    )(small)


def _fox_prompt_kernel(q_ref, k_ref, vt_ref, f_ref, o_ref, m_sc, l_sc, acc_sc, *, g, tq, scale):
    qi = pl.program_id(2)
    m_sc[...] = jnp.full(m_sc.shape, NEG_INF, F32)
    l_sc[...] = jnp.zeros(l_sc.shape, F32)
    acc_sc[...] = jnp.zeros(acc_sc.shape, F32)
    qs = jnp.concatenate([q_ref[:, i * LANE:(i + 1) * LANE] for i in range(g)], axis=0)

    def block(ki, masked):
        ks = pl.ds(pl.multiple_of(ki * tq, tq), tq)
        st = lax.dot_general(k_ref[ks, :], qs, NT, preferred_element_type=F32) * scale
        st = st - jnp.concatenate([jnp.broadcast_to(f_ref[0, 0, ks, i:i + 1], (tq, tq)) for i in range(g)], axis=1)
        if masked:
            keep = lax.broadcasted_iota(jnp.int32, (tq, tq), 0) <= lax.broadcasted_iota(jnp.int32, (tq, tq), 1)
            st = jnp.where(jnp.concatenate([keep] * g, axis=1), st, NEG_INF)
        m_prev = m_sc[...]
        m_new = jnp.maximum(m_prev, jnp.max(st, axis=0, keepdims=True))
        alpha = jnp.exp(m_prev - m_new)
        p = jnp.exp(st - m_new)
        l_sc[...] = alpha * l_sc[...] + jnp.sum(p, axis=0, keepdims=True)
        acc_sc[...] = alpha * acc_sc[...] + jnp.dot(vt_ref[:, ks], p.astype(BF16), preferred_element_type=F32)
        m_sc[...] = m_new

    def body(ki, carry):
        block(ki, False)
        return carry

    lax.fori_loop(0, qi, body, 0)
    block(qi, True)
    ot = acc_sc[...] / l_sc[...]
    for i in range(g):
        o_ref[:, i * LANE:(i + 1) * LANE] = ot[:, i * tq:(i + 1) * tq].T.astype(o_ref.dtype)


def _fox_prompt(qn, kn, vt, fcol, nb, t, nkv, g):
    tq = _pick(t, 512, LANE)
    nq = t // tq
    n = g * tq
    return pl.pallas_call(
        functools.partial(_fox_prompt_kernel, g=g, tq=tq, scale=LANE ** -0.5),
        grid=(nb, nkv, nq),
        in_specs=[pl.BlockSpec((tq, g * LANE), lambda b, h, qi: (b * nq + qi, h)),
                  pl.BlockSpec((t, LANE), lambda b, h, qi: (b, h)),
                  pl.BlockSpec((LANE, t), lambda b, h, qi: (h, b)),
                  pl.BlockSpec((1, 1, t, LANE), lambda b, h, qi: (b, h, 0, 0))],
        out_specs=pl.BlockSpec((tq, g * LANE), lambda b, h, qi: (b * nq + qi, h)),
        out_shape=jax.ShapeDtypeStruct((nb * t, nkv * g * LANE), BF16),
        scratch_shapes=[pltpu.VMEM((1, n), F32), pltpu.VMEM((1, n), F32), pltpu.VMEM((LANE, n), F32)],
        compiler_params=_params("parallel", "parallel", "arbitrary"),
        name="fox_prompt",
    )(qn, kn, vt, fcol)


def _fox_sample_kernel(pt_ref, q_ref, kn_ref, vn_ref, lfn_ref, *refs, npg, nkv, g, ts, scale):
    k_refs, v_refs, lf_refs = refs[:npg], refs[npg:2 * npg], refs[2 * npg:3 * npg]
    o_ref, m_sc, l_sc, acc_sc, f_sc = refs[3 * npg:]
    p = pl.program_id(1)
    page = k_refs[0].shape[1]
    nh = nkv * g
    rows = ts * nh
    cols = page * nkv

    @pl.when(p == 0)
    def _():
        m_sc[...] = jnp.full(m_sc.shape, NEG_INF, F32)
        l_sc[...] = jnp.zeros(l_sc.shape, F32)
        acc_sc[...] = jnp.zeros(acc_sc.shape, F32)
        f_sc[...] = jnp.zeros(f_sc.shape, F32)

    q = q_ref[0]

    def update(s, pv):
        m_prev = m_sc[...]
        m_new = jnp.maximum(m_prev, jnp.max(s, axis=-1, keepdims=True))
        alpha = jnp.exp(m_prev - m_new)
        pr = jnp.exp(s - m_new)
        l_sc[...] = alpha * l_sc[...] + jnp.sum(pr, axis=-1, keepdims=True)
        acc_sc[...] = alpha * acc_sc[...] + pv(pr.astype(BF16))
        m_sc[...] = m_new

    def own_kv(width):
        rq = lax.broadcasted_iota(jnp.int32, (rows, width), 0)
        cq = lax.broadcasted_iota(jnp.int32, (rows, width), 1)
        return rq, cq, (cq % nkv) == ((rq % nh) // g)

    ue = (lax.broadcasted_iota(jnp.int32, (page, cols), 0)
          <= lax.broadcasted_iota(jnp.int32, (page, cols), 1) // nkv).astype(BF16)
    _, _, own = own_kv(cols)
    carry = f_sc[...]
    parts = []
    for i in range(npg):
        fexp = _dot_sel(ue, lf_refs[i][0], sel_first=False) + carry[:, 0:1]
        carry = jnp.broadcast_to(fexp[:, cols - 1:cols], carry.shape)
        kf = k_refs[i][0].reshape(cols, LANE).astype(BF16)
        s = lax.dot_general(q, kf, NT, preferred_element_type=F32) * scale - jnp.concatenate([fexp] * ts, axis=0)
        parts.append(jnp.where(own, s, NEG_INF))
    f_sc[...] = carry

    def pv_pages(pr):
        out = None
        for i in range(npg):
            t_ = jnp.dot(pr[:, i * cols:(i + 1) * cols], v_refs[i][0].reshape(cols, LANE).astype(BF16),
                         preferred_element_type=F32)
            out = t_ if out is None else out + t_
        return out

    update(jnp.concatenate(parts, axis=1), pv_pages)

    @pl.when(p == pl.num_programs(1) - 1)
    def _():
        nn = kn_ref.shape[1]
        nl = lfn_ref.shape[1]
        uen = (lax.broadcasted_iota(jnp.int32, (nl, nn), 0)
               <= lax.broadcasted_iota(jnp.int32, (nl, nn), 1) // nkv).astype(BF16)
        fnew = _dot_sel(uen, lfn_ref[0], sel_first=False, dims=TN)[:nh] + f_sc[...][:, 0:1]
        rq, cq, own_n = own_kv(nn)
        keep = own_n & ((cq // nkv) <= (rq // nh)) & (cq < ts * nkv)
        s = lax.dot_general(q, kn_ref[0], NT, preferred_element_type=F32) * scale - jnp.concatenate([fnew] * ts, axis=0)
        update(jnp.where(keep, s, NEG_INF), lambda pr: jnp.dot(pr, vn_ref[0], preferred_element_type=F32))
        o_ref[0] = (acc_sc[...] / l_sc[...]).astype(o_ref.dtype)


def _fox_sample(page_ids, qb, knew, vnew, lfnew, kc, vc, lfc, nkv, g, ts):
    bs, rows, _ = qb.shape
    npages = page_ids.shape[1]
    npg = _pick(npages, 16, 1)
    page = kc.shape[1]
    nh = lfc.shape[1]
    nn = knew.shape[1]
    nl = lfnew.shape[1]

    def kv_page(i):
        return pl.BlockSpec((1, page, nkv, LANE), lambda b, p, pt, i=i: (pt[b * npages + p * npg + i], 0, 0, 0))

    def lf_page(i):
        return pl.BlockSpec((1, nh, page), lambda b, p, pt, i=i: (pt[b * npages + p * npg + i], 0, 0))

    per_b = lambda r: pl.BlockSpec((1, r, LANE), lambda b, p, pt: (b, 0, 0))
    in_specs = [per_b(rows), per_b(nn), per_b(nn), per_b(nl)]
    in_specs += [kv_page(i) for i in range(npg)] + [kv_page(i) for i in range(npg)] + [lf_page(i) for i in range(npg)]
    grid_spec = pltpu.PrefetchScalarGridSpec(
        num_scalar_prefetch=1,
        grid=(bs, npages // npg),
        in_specs=in_specs,
        out_specs=per_b(rows),
        scratch_shapes=[pltpu.VMEM((rows, 1), F32), pltpu.VMEM((rows, 1), F32), pltpu.VMEM((rows, LANE), F32),
                        pltpu.VMEM((nh, LANE), F32)],
    )
    return pl.pallas_call(
        functools.partial(_fox_sample_kernel, npg=npg, nkv=nkv, g=g, ts=ts, scale=LANE ** -0.5),
        grid_spec=grid_spec,
        out_shape=jax.ShapeDtypeStruct((bs, rows, LANE), BF16),
        compiler_params=_params("parallel", "arbitrary"),
        name="fox_sample",
    )(page_ids.reshape(-1), qb, knew, vnew, lfnew, *([kc] * npg), *([vc] * npg), *([lfc] * npg))


def _gdn_kernel(*refs, hb, c, taps, use_state, scale):
    if use_state:
        (q_ref, k_ref, v_ref, wq_ref, wk_ref, wv_ref, z_ref, gb_ref, nw_ref, hq_ref, hk_ref, hv_ref, s0_ref,
         o_ref, so_ref, s_sc, tail_sc) = refs
    else:
        q_ref, k_ref, v_ref, wq_ref, wk_ref, wv_ref, z_ref, gb_ref, nw_ref, o_ref, so_ref, s_sc, tail_sc = refs
    n = pl.program_id(2)

    @pl.when(n == 0)
    def _():
        s_sc[...] = s0_ref[0] if use_state else jnp.zeros(s_sc.shape, F32)
        if use_state:
            for i, h_ref in enumerate((hq_ref, hk_ref, hv_ref)):
                tail_sc[i] = h_ref[...]
        else:
            tail_sc[...] = jnp.zeros(tail_sc.shape, F32)

    heads = range(hb)

    def conv_act(x_ref, w_ref, idx, l2_scale):
        x = x_ref[...]
        ext = jnp.concatenate([tail_sc[idx], x], axis=0)
        tail_sc[idx] = x[c - SUBLANE:c]
        y = None
        for tap in range(taps):
            off = SUBLANE - (taps - 1 - tap)
            term = ext[off:off + c] * w_ref[tap:tap + 1, :]
            y = term if y is None else y + term
        y = _silu(y)
        cols = [y[:, h * LANE:(h + 1) * LANE] for h in heads]
        if l2_scale is not None:
            cols = [yh * (lax.rsqrt(jnp.sum(yh * yh, axis=-1, keepdims=True) + L2_EPS) * l2_scale) for yh in cols]
        return jnp.stack(cols)

    q = conv_act(q_ref, wq_ref, 0, scale)
    k = conv_act(k_ref, wk_ref, 1, 1.0)
    v = conv_act(v_ref, wv_ref, 2, None)
    gb = gb_ref[0]
    r = lax.broadcasted_iota(jnp.int32, (c, c), 0)
    cc = lax.broadcasted_iota(jnp.int32, (c, c), 1)
    incl, strict = r >= cc, r > cc
    gc_all = _dot_sel(incl.astype(BF16), gb)
    gc_t = gc_all.T
    gcol = jnp.stack([gc_all[:, h:h + 1] for h in heads])
    grow = jnp.stack([gc_t[h:h + 1, :] for h in heads])
    beta = jnp.stack([gb[:, hb + h:hb + h + 1] for h in heads])
    glast = gcol[:, c - 1:c, :]
    decay = jnp.exp(jnp.where(incl[None], gcol - grow, NEG_INF))
    kb = k * beta
    bdot = lambda a, b, eq: jnp.einsum(eq, a.astype(BF16), b.astype(BF16), preferred_element_type=F32)

    mm = jnp.where(strict[None], bdot(kb, k, 'hck,hsk->hcs') * decay, 0.0)
    eye = (r == cc).astype(F32)[None]
    base = min(c, SUBLANE)
    same = lambda s: ((r // s) == (cc // s))[None]
    mp = jnp.where(same(base), mm, 0.0)
    tinv = eye - mp
    span = 2
    while span < base:
        mp = bdot(mp, mp, 'hcs,hsk->hck')
        tinv = bdot(tinv, eye + mp, 'hcs,hsk->hck')
        span *= 2
    blk = base
    while blk < c:
        off = jnp.where(same(2 * blk) & ~same(blk), mm, 0.0)
        tinv = tinv - bdot(tinv, bdot(off, tinv, 'hcs,hsk->hck'), 'hcs,hsk->hck')
        blk *= 2
    egc = jnp.exp(gcol)
    u = bdot(tinv, v * beta, 'hcs,hsv->hcv')
    w = bdot(tinv, kb * egc, 'hcs,hsk->hck')
    a = bdot(q, k, 'hck,hsk->hcs') * decay
    qd = q * egc
    kd = k * jnp.exp(glast - gcol)
    s = s_sc[...]
    v_new = u - bdot(w, s, 'hck,hkv->hcv')
    o = bdot(qd, s, 'hck,hkv->hcv') + bdot(a, v_new, 'hcs,hsv->hcv')
    s_new = s * jnp.exp(glast) + bdot(kd, v_new, 'hck,hcv->hkv')
    s_sc[...] = s_new

    @pl.when(n == pl.num_programs(2) - 1)
    def _():
        so_ref[0] = s_new

    on = o * lax.rsqrt(jnp.mean(o * o, axis=-1, keepdims=True) + RMS_EPS) * nw_ref[...][None]
    for h in heads:
        sl = slice(h * LANE, (h + 1) * LANE)
        o_ref[:, sl] = (on[h] * _silu(z_ref[:, sl])).astype(o_ref.dtype)


def _gdn(x, conv_w, gb, norm_w, hist, s0, nb, tseq, c, hb, nheads):
    m = x.shape[0]
    n = tseq // c
    w = hb * LANE
    ng = nheads // hb
    kw = nheads * LANE
    taps = conv_w.shape[0]
    row_col = lambda cb: pl.BlockSpec((c, w), lambda b, hg, i, cb=cb: (b * n + i, cb + hg))
    wcol = lambda cb: pl.BlockSpec((taps, w), lambda b, hg, i, cb=cb: (0, cb + hg))
    in_specs = [row_col(0), row_col(kw // w), row_col(2 * kw // w),
                wcol(0), wcol(kw // w), wcol(2 * kw // w),
                row_col(3 * kw // w),
                pl.BlockSpec((1, c, LANE), lambda b, hg, i: (hg, b * n + i, 0)),
                pl.BlockSpec((1, LANE), lambda b, hg, i: (0, 0))]
    args = [x, x, x, conv_w, conv_w, conv_w, x, gb, norm_w.reshape(1, LANE)]
    state_spec = pl.BlockSpec((1, hb, LANE, LANE), lambda b, hg, i: (b, hg, 0, 0))
    if s0 is not None:
        hcol = lambda cb: pl.BlockSpec((SUBLANE, w), lambda b, hg, i, cb=cb: (b, cb + hg))
        in_specs += [hcol(0), hcol(kw // w), hcol(2 * kw // w), state_spec]
        args += [hist, hist, hist, s0]
    return pl.pallas_call(
        functools.partial(_gdn_kernel, hb=hb, c=c, taps=taps, use_state=s0 is not None, scale=LANE ** -0.5),
        grid=(nb, ng, n),
        in_specs=in_specs,
        out_specs=[pl.BlockSpec((c, w), lambda b, hg, i: (b * n + i, hg)), state_spec],
        out_shape=[jax.ShapeDtypeStruct((m, kw), BF16), jax.ShapeDtypeStruct((nb, nheads, LANE, LANE), F32)],
        scratch_shapes=[pltpu.VMEM((hb, LANE, LANE), F32), pltpu.VMEM((3, SUBLANE, w), F32)],
        compiler_params=_params("parallel", "parallel", "arbitrary"),
        name="gdn",
    )(*args)


def _merge_kernel(of_ref, og_ref, wf_ref, wg_ref, gf_ref, gg_ref, o_ref):
    a = jnp.dot(of_ref[...], wf_ref[...].astype(BF16), preferred_element_type=F32)
    b = jnp.dot(og_ref[...], wg_ref[...].astype(BF16), preferred_element_type=F32)
    o_ref[...] = (jax.nn.sigmoid(gf_ref[...]) * a + jax.nn.sigmoid(gg_ref[...]) * b).astype(o_ref.dtype)


def _merge(o_fox, o_gdn, w_bf, w_bg, proj, gf_col0, gg_col0, tm, tn):
    m, kf = o_fox.shape
    kg = o_gdn.shape[1]
    n = w_bf.shape[1]
    return pl.pallas_call(
        _merge_kernel,
        grid=(m // tm, n // tn),
        in_specs=[pl.BlockSpec((tm, kf), lambda i, j: (i, 0)),
                  pl.BlockSpec((tm, kg), lambda i, j: (i, 0)),
                  pl.BlockSpec((kf, tn), lambda i, j: (0, j)),
                  pl.BlockSpec((kg, tn), lambda i, j: (0, j)),
                  pl.BlockSpec((tm, tn), lambda i, j: (i, gf_col0 // tn + j)),
                  pl.BlockSpec((tm, tn), lambda i, j: (i, gg_col0 // tn + j))],
        out_specs=pl.BlockSpec((tm, tn), lambda i, j: (i, j)),
        out_shape=jax.ShapeDtypeStruct((m, n), BF16),
        compiler_params=_params("parallel", "arbitrary"),
        name="merge",
    )(o_fox, o_gdn, w_bf, w_bg, proj, proj)


def _resid_kernel(a_ref, w_ref, x_ref, gt_ref, o_ref):
    acc = jnp.dot(a_ref[...], w_ref[...].astype(BF16), preferred_element_type=F32)
    o_ref[0] = x_ref[0] + _rows(acc, gt_ref[0]) * acc


def _resid_matmul(a, w, x3, gt3, tiles_per_mod, tn, name):
    nt, tm, n = x3.shape
    k = a.shape[1]
    mr = gt3.shape[1]
    return pl.pallas_call(
        _resid_kernel,
        grid=(nt, n // tn),
        in_specs=[pl.BlockSpec((tm, k), lambda i, j: (i, 0)),
                  pl.BlockSpec((k, tn), lambda i, j: (0, j)),
                  pl.BlockSpec((1, tm, tn), lambda i, j: (i, 0, j)),
                  pl.BlockSpec((1, mr, tn), lambda i, j: (i // tiles_per_mod, 0, j))],
        out_specs=pl.BlockSpec((1, tm, tn), lambda i, j: (i, 0, j)),
        out_shape=jax.ShapeDtypeStruct((nt, tm, n), F32),
        compiler_params=_params("parallel", "arbitrary"),
        name=name,
    )(a, w, x3, gt3)


def _ffn_up_kernel(*refs, stride, taps, tiles_per_seq, halo_from_h):
    if halo_from_h:
        hh_ref, h_ref, wg_ref, wu_ref, cw_ref, act_ref, tail_ref, hcat = refs
    else:
        hist_ref, h_ref, wg_ref, wu_ref, cw_ref, act_ref, tail_ref = refs
    i, j = pl.program_id(0), pl.program_id(1)
    tm = h_ref.shape[0]
    if halo_from_h:
        hh = hh_ref.shape[0]

        @pl.when(j == 0)
        def _():
            hcat[0:hh, :] = jnp.where(i % tiles_per_seq == 0, jnp.zeros_like(hh_ref[...]), hh_ref[...])
            hcat[hh:hh + tm, :] = h_ref[...]

        ext = jnp.dot(hcat[...], wg_ref[...].astype(BF16), preferred_element_type=F32)
    else:
        hh = hist_ref.shape[0]
        ext = jnp.concatenate(
            [hist_ref[...], jnp.dot(h_ref[...], wg_ref[...].astype(BF16), preferred_element_type=F32)], axis=0)
    up = jnp.dot(h_ref[...], wu_ref[...].astype(BF16), preferred_element_type=F32)
    y = None
    for tap in range(taps):
        off = hh - (taps - 1 - tap) * stride
        term = ext[off:off + tm] * cw_ref[tap:tap + 1, :]
        y = term if y is None else y + term
    act_ref[...] = (_silu(y) * up).astype(act_ref.dtype)
    tail_ref[0] = ext[hh + tm - tail_ref.shape[1]:hh + tm]


def _ffn_up(h, hist, w_gate, w_up, conv_w, tm, tn, stride, tiles_per_seq, tail_rows):
    m, k = h.shape
    n = w_gate.shape[1]
    taps = conv_w.shape[0]
    halo_from_h = hist is None
    if halo_from_h:
        hh = BF16_ROWS
        first = pl.BlockSpec((hh, k), lambda i, j: (jnp.maximum(i * (tm // hh) - 1, 0), 0))
        first_arg = h
        scratch = [pltpu.VMEM((hh + tm, k), BF16)]
    else:
        hh = hist.shape[0]
        first = pl.BlockSpec((hh, tn), lambda i, j: (0, j))
        first_arg = hist
        scratch = []
    return pl.pallas_call(
        functools.partial(_ffn_up_kernel, stride=stride, taps=taps, tiles_per_seq=tiles_per_seq,
                          halo_from_h=halo_from_h),
        grid=(m // tm, n // tn),
        in_specs=[first,
                  pl.BlockSpec((tm, k), lambda i, j: (i, 0)),
                  pl.BlockSpec((k, tn), lambda i, j: (0, j)),
                  pl.BlockSpec((k, tn), lambda i, j: (0, j)),
                  pl.BlockSpec((taps, tn), lambda i, j: (0, j))],
        out_specs=[pl.BlockSpec((tm, tn), lambda i, j: (i, j)),
                   pl.BlockSpec((1, tail_rows, tn), lambda i, j: (i, 0, j))],
        out_shape=[jax.ShapeDtypeStruct((m, n), BF16), jax.ShapeDtypeStruct((m // tm, tail_rows, n), F32)],
        scratch_shapes=scratch,
        compiler_params=_params("parallel", "arbitrary"),
        name="ffn_up",
    )(first_arg, h, w_gate, w_up, conv_w)


def _layer(x2, mods, wts, lay, *, prompt, nb, tseq, fox_fn, gdn_s0, gdn_hist, ffn_hist):
    m, d = x2.shape
    sh_m, sc_m, gt_m, sh_f, sc_f, gt_f = mods
    nh, nkv, g, gh = lay["nh"], lay["nkv"], lay["g"], lay["gh"]
    kw = gh * LANE
    if prompt:
        tr = _pick(tseq, 256, SUBLANE)
        tm = _pick(tseq, 1024, LANE)
        stride = 1
    else:
        tr = nb
        tm = m
        stride = nb
    tiles_mod = (tseq // tr) if prompt else m // tr
    h = _normmod(x2.reshape(m // tr, tr, d), wts["norm_mix"], sc_m, sh_m, tiles_mod).reshape(m, d)
    mm = lambda wt: _matmul(h, wt, tm, _pick(wt.shape[0], 1024, LANE))
    proj_fox, proj_gdn, proj_gate = mm(wts["w_fox"]), mm(wts["w_gdn"]), mm(wts["w_gate"])
    small = _small_heads(h, wts["w_small"], wts["p_small"], nh, tm)
    qn, kn, vb, vt, k_f, v_f = _qknorm(proj_fox, wts["fox_q_norm"], wts["fox_k_norm"], nh, nkv, _pick(m, 256, LANE))
    o_fox = fox_fn(qn, kn, vb, vt, small)
    hb = lay["hb"]
    if prompt:
        o_gdn, s_fin = _gdn(proj_gdn, wts["gdn_conv_w"], _group_gb(small, nh, gh, hb), wts["gdn_norm"], None, None,
                            nb, tseq, lay["chunk"], hb, gh)
    else:
        ts = m // nb
        cpad = lay["cpad"]
        to_bm = lambda a: jnp.pad(a.reshape(ts, nb, -1).transpose(1, 0, 2),
                                  ((0, 0), (0, cpad - ts), (0, 0))).reshape(nb * cpad, -1)
        o_bm, s_fin = _gdn(to_bm(proj_gdn), wts["gdn_conv_w"], _group_gb(to_bm(small), nh, gh, hb), wts["gdn_norm"],
                           gdn_hist, gdn_s0, nb, cpad, cpad, hb, gh)
        o_gdn = o_bm.reshape(nb, cpad, kw)[:, :ts].transpose(1, 0, 2).reshape(m, kw)
    tn = _pick(d, 512, LANE)
    merged = _merge(o_fox, o_gdn, wts["w_branch_fox"], wts["w_branch_gdn"], proj_gate, 0, d, tm, tn)
    tiles_mod_m = (tseq // tm) if prompt else 1
    x3 = _resid_matmul(merged, wts["w_out"], x2.reshape(m // tm, tm, d), gt_m, tiles_mod_m, tn, "out_proj")
    x2 = x3.reshape(m, d)
    h2 = _normmod(x2.reshape(m // tr, tr, d), wts["norm_ffn"], sc_f, sh_f, tiles_mod).reshape(m, d)
    dff = wts["ffn_w_gate"].shape[1]
    tnf = _pick(dff, 256, LANE)
    act, tail = _ffn_up(h2, ffn_hist, wts["ffn_w_gate"], wts["ffn_w_up"], wts["ffn_conv_w"], tm, tnf, stride,
                        tseq // tm if prompt else 1, SUBLANE if prompt else m)
    tm2 = _pick(tm, 512, LANE) if prompt else m
    y3 = _resid_matmul(act, wts["ffn_w_down"], x2.reshape(m // tm2, tm2, d), gt_f,
                       (tseq // tm2) if prompt else 1, _pick(d, 512, LANE), "ffn_down")
    return y3.reshape(m, d), dict(gdn=proj_gdn, small=small, k=k_f, v=v_f, s=s_fin, tail=tail)


def _group_gb(small, nh, gh, hb):
    rows = small.shape[0]
    gg = small[:, nh:nh + gh].reshape(rows, gh // hb, hb)
    bb = small[:, nh + gh:nh + 2 * gh].reshape(rows, gh // hb, hb)
    gb = jnp.concatenate([gg, bb], axis=-1).transpose(1, 0, 2)
    return jnp.pad(gb, ((0, 0), (0, 0), (0, LANE - 2 * hb)))


def kernel(x_prompt, x_sample, cache_k, cache_v, cache_logf, state_gdn, state_gdn_conv, state_ffn_conv, page_table, c_prompt, c_sample, w_ada, b_ada, norm_mix, norm_ffn, w_in, fox_b_f, fox_q_norm, fox_k_norm, gdn_conv_w, gdn_A_log, gdn_dt_bias, gdn_norm, w_branch_fox, w_branch_gdn, w_out, ffn_w_gate, ffn_w_up, ffn_conv_w, ffn_w_down):
    nb, t, d = x_prompt.shape
    bs, ts, _ = x_sample.shape
    depth = w_in.shape[0]
    page, nkv = cache_k.shape[2], cache_k.shape[3]
    nh = cache_logf.shape[-1]
    g = nh // nkv
    gh = state_gdn.shape[2]
    kw = gh * LANE
    gconv = gdn_conv_w.shape[1]
    fconv = ffn_conv_w.shape[1]
    dff = ffn_w_gate.shape[-1]
    assert cache_k.shape[-1] == LANE and state_gdn.shape[-1] == LANE and state_gdn.shape[-2] == LANE
    assert gh == nh and ts >= gconv - 1 and ts >= fconv - 1

    sizes = (nh * LANE, nkv * LANE, nkv * LANE, nh, 3 * kw, kw, gh, gh, d, d)
    starts = [0]
    for s_ in sizes:
        starts.append(starts[-1] + s_)
    assert starts[-1] == w_in.shape[-1]
    hb = _pick(gh, 16, 1)
    lay = dict(nh=nh, nkv=nkv, g=g, gh=gh, hb=hb, chunk=_pick(t, 64, SUBLANE), cpad=SUBLANE)

    xs = x_sample.transpose(1, 0, 2).reshape(ts * bs, d)
    xp = x_prompt.reshape(nb * t, d)
    mc = -(-nb // SUBLANE) * SUBLANE
    c_all = jnp.concatenate([c_prompt, jnp.zeros((mc - nb, d), F32), c_sample], axis=0)

    outs_p = [[] for _ in range(6)]
    outs_s = [[] for _ in range(6)]
    for l in range(depth):
        nin = w_in.shape[2]
        wt = jnp.swapaxes(w_in, 1, 2).reshape(depth * nin, w_in.shape[1])
        run = lambda a, b: _wprep(wt, l * nin + starts[a], starts[b] - starts[a])
        w_small = _wsmall(wt, l * nin + starts[3], l * nin + starts[6], nh)
        zpad = jnp.zeros((LANE - 2 * nh,), F32)
        p_small = jnp.stack([jnp.concatenate([fox_b_f[l], gdn_dt_bias[l], zpad]),
                             jnp.concatenate([jnp.zeros((nh,), F32), gdn_A_log[l], zpad])])
        p_small = jnp.pad(p_small, ((0, SUBLANE - 2), (0, 0)))
        wts = dict(norm_mix=norm_mix[l], norm_ffn=norm_ffn[l], w_fox=run(0, 3), w_gdn=run(4, 6), w_gate=run(8, 10),
                   w_small=w_small, p_small=p_small,
                   fox_q_norm=fox_q_norm[l], fox_k_norm=fox_k_norm[l], gdn_conv_w=gdn_conv_w[l],
                   gdn_norm=gdn_norm[l], w_branch_fox=w_branch_fox[l], w_branch_gdn=w_branch_gdn[l],
                   w_out=w_out[l], ffn_w_gate=ffn_w_gate[l], ffn_w_up=ffn_w_up[l],
                   ffn_conv_w=ffn_conv_w[l], ffn_w_down=ffn_w_down[l].astype(BF16))

        mod = _ada(c_all, w_ada[l], b_ada[l])
        mods_p = [mod[:nb, i * d:(i + 1) * d].reshape(nb, 1, d) for i in range(6)]
        mods_s = [mod[mc:, i * d:(i + 1) * d].reshape(1, bs, d) for i in range(6)]

        def fox_p(qn, kn, vb, vt, small):
            f_tm = _cumf(small, nb, t)
            fcol = f_tm[:, :nh].reshape(nb, t, nkv, g).transpose(0, 2, 1, 3)
            fcol = jnp.pad(fcol, ((0, 0), (0, 0), (0, 0), (0, LANE - g)))
            return _fox_prompt(qn, kn, vt, fcol, nb, t, nkv, g)

        xp, st = _layer(xp, mods_p, wts, lay, prompt=True, nb=nb, tseq=t, fox_fn=fox_p, gdn_s0=None,
                        gdn_hist=None, ffn_hist=None)
        outs_p[0].append(st["k"].reshape(nb, t, nkv, LANE))
        outs_p[1].append(st["v"].reshape(nb, t, nkv, LANE))
        outs_p[2].append(st["small"].reshape(nb, t, LANE)[:, :, :nh])
        outs_p[3].append(st["s"])
        outs_p[4].append(st["gdn"].reshape(nb, t, -1)[:, t - (gconv - 1):, :3 * kw])
        tail = st["tail"].reshape(nb, -1, SUBLANE, dff)
        outs_p[5].append(tail[:, -1, SUBLANE - (fconv - 1):, :])

        n_pool = cache_k.shape[1]
        kc = cache_k.reshape(depth * n_pool, page, nkv, LANE)
        vc = cache_v.reshape(depth * n_pool, page, nkv, LANE)
        lfc = jnp.swapaxes(cache_logf, 2, 3).reshape(depth * n_pool, nh, page)
        page_ids = page_table + l * n_pool

        def fox_s(qn, kn, vb, vt, small):
            bm = lambda a, n: a.reshape(ts, bs, n, LANE).transpose(1, 0, 2, 3).reshape(bs, ts * n, LANE)
            nn = -(-ts * nkv // BF16_ROWS) * BF16_ROWS
            padr = lambda a, n: jnp.pad(a, ((0, 0), (0, n - a.shape[1]), (0, 0)))
            lfn = padr(small.reshape(ts, bs, LANE).transpose(1, 0, 2), BF16_ROWS)
            o = _fox_sample(page_ids, bm(qn, nh), padr(bm(kn, nkv), nn), padr(bm(vb, nkv), nn), lfn, kc, vc, lfc,
                            nkv, g, ts)
            return o.reshape(bs, ts, nh, LANE).transpose(1, 0, 2, 3).reshape(ts * bs, nh * LANE)

        ghist = jnp.pad(state_gdn_conv[l], ((0, 0), (SUBLANE - (gconv - 1), 0), (0, 0))).reshape(bs * SUBLANE, 3 * kw)
        fhist = state_ffn_conv[l].transpose(1, 0, 2).reshape((fconv - 1) * bs, dff)
        xs, st = _layer(xs, mods_s, wts, lay, prompt=False, nb=bs, tseq=ts, fox_fn=fox_s, gdn_s0=state_gdn[l],
                        gdn_hist=ghist, ffn_hist=fhist)
        bm3 = lambda a: a.reshape(ts, bs, -1).transpose(1, 0, 2)
        outs_s[0].append(bm3(st["k"]).reshape(bs, ts, nkv, LANE))
        outs_s[1].append(bm3(st["v"]).reshape(bs, ts, nkv, LANE))
        outs_s[2].append(bm3(st["small"])[:, :, :nh])
        outs_s[3].append(st["s"])
        graw = bm3(st["gdn"])[:, :, :3 * kw]
        outs_s[4].append(jnp.concatenate([state_gdn_conv[l], graw], axis=1)[:, -(gconv - 1):])
        fraw = bm3(st["tail"].reshape(ts * bs, dff))
        outs_s[5].append(jnp.concatenate([state_ffn_conv[l], fraw], axis=1)[:, -(fconv - 1):])

    y_prompt = xp.reshape(nb, t, d)
    y_sample = xs.reshape(ts, bs, d).transpose(1, 0, 2)
    return (y_prompt, y_sample, *(jnp.stack(a) for a in outs_p), *(jnp.stack(a) for a in outs_s))
```

```python
import functools

import jax
import jax.numpy as jnp
from jax import lax
from jax.experimental import pallas as pl
from jax.experimental.pallas import tpu as pltpu

F32 = jnp.float32
BF16 = jnp.bfloat16
RMS_EPS = 1e-6
L2_EPS = 1e-6
NEG_INF = -1e30
LANE = 128
SUBLANE = 8
BF16_ROWS = 16
VMEM_LIMIT_BYTES = 56 * 2 ** 20

NN = (((1,), (0,)), ((), ()))
NT = (((1,), (1,)), ((), ()))
TN = (((0,), (0,)), ((), ()))


def _pick(n, pref, align):
    t = min(pref, n) // align * align
    while t >= align:
        if n % t == 0:
            return t
        t -= align
    return n


def _params(*sem):
    return pltpu.CompilerParams(dimension_semantics=sem, vmem_limit_bytes=VMEM_LIMIT_BYTES)


def _dot(a, b, dims=NN):
    return lax.dot_general(a.astype(BF16), b.astype(BF16), dims, preferred_element_type=F32)


def _dot_sel(sel, x, sel_first=True, dims=NN):
    h = x.astype(BF16)
    r = x - h.astype(F32)
    m = r.astype(BF16)
    l = (r - m.astype(F32)).astype(BF16)
    if sel_first:
        d = lambda p: lax.dot_general(sel, p, dims, preferred_element_type=F32)
    else:
        d = lambda p: lax.dot_general(p, sel, dims, preferred_element_type=F32)
    return d(h) + (d(m) + d(l))


def _silu(x):
    return x * jax.nn.sigmoid(x)


def _softplus(x):
    return jnp.maximum(x, 0.0) + jnp.log1p(jnp.exp(-jnp.abs(x)))


def _rows(a, m):
    if m.shape[0] == 1 or m.shape[0] == a.shape[0]:
        return m
    reps = a.shape[0] // m.shape[0]
    return jnp.concatenate([m] * reps, axis=0)


def _ada_kernel(c_ref, w_ref, b_ref, o_ref):
    a = _silu(c_ref[...])
    o_ref[...] = _dot(a, w_ref[...]) + b_ref[...]


def _ada(c_all, w_ada, b_ada):
    mc, d = c_all.shape
    n = w_ada.shape[1]
    tn = _pick(n, 512, LANE)
    return pl.pallas_call(
        _ada_kernel,
        grid=(n // tn,),
        in_specs=[pl.BlockSpec((mc, d), lambda j: (0, 0)),
                  pl.BlockSpec((d, tn), lambda j: (0, j)),
                  pl.BlockSpec((1, tn), lambda j: (0, j))],
        out_specs=pl.BlockSpec((mc, tn), lambda j: (0, j)),
        out_shape=jax.ShapeDtypeStruct((mc, n), F32),
        compiler_params=_params("parallel"),
        name="ada",
    )(c_all, w_ada, b_ada.reshape(1, n))


def _normmod_kernel(x_ref, g_ref, sc_ref, sh_ref, o_ref):
    x = x_ref[0]
    y = x * lax.rsqrt(jnp.mean(x * x, axis=-1, keepdims=True) + RMS_EPS) * g_ref[...]
    o_ref[0] = (y * (1.0 + _rows(y, sc_ref[0])) + _rows(y, sh_ref[0])).astype(o_ref.dtype)


def _normmod(x3, gain, sc3, sh3, tiles_per_mod):
    nt, tr, d = x3.shape
    mr = sc3.shape[1]
    mod_spec = pl.BlockSpec((1, mr, d), lambda i: (i // tiles_per_mod, 0, 0))
    return pl.pallas_call(
        _normmod_kernel,
        grid=(nt,),
        in_specs=[pl.BlockSpec((1, tr, d), lambda i: (i, 0, 0)),
                  pl.BlockSpec((1, d), lambda i: (0, 0)),
                  mod_spec, mod_spec],
        out_specs=pl.BlockSpec((1, tr, d), lambda i: (i, 0, 0)),
        out_shape=jax.ShapeDtypeStruct((nt, tr, d), BF16),
        compiler_params=_params("parallel"),
        name="normmod",
    )(x3, gain.reshape(1, d), sc3, sh3)


def _wprep_kernel(a_ref, b_ref, o_ref, *, delta):
    a = a_ref[...]
    if delta:
        a = jnp.concatenate([a[delta:], b_ref[0:delta, :]], axis=0)
    o_ref[...] = a.astype(BF16)


def _wprep(wt, row0, rows):
    k = wt.shape[1]
    tr = _pick(rows, 512, LANE)
    tk = _pick(k, 2048, LANE)
    delta, rb = row0 % tr, row0 // tr
    assert delta < LANE
    nxt = (lambda i, j: ((rb + i + 1) * (tr // LANE), j)) if delta else (lambda i, j: (0, j))
    return pl.pallas_call(
        functools.partial(_wprep_kernel, delta=delta),
        grid=(rows // tr, k // tk),
        in_specs=[pl.BlockSpec((tr, tk), lambda i, j: (rb + i, j)), pl.BlockSpec((LANE, tk), nxt)],
        out_specs=pl.BlockSpec((tr, tk), lambda i, j: (i, j)),
        out_shape=jax.ShapeDtypeStruct((rows, k), BF16),
        compiler_params=_params("parallel", "parallel"),
        name="wprep",
    )(wt, wt)


def _wsmall_kernel(a_ref, b_ref, o_ref, *, nh):
    row = lax.broadcasted_iota(jnp.int32, a_ref.shape, 0)
    o_ref[...] = jnp.where(row < nh, a_ref[...], jnp.where(row < 3 * nh, b_ref[...], 0.0)).astype(BF16)


def _wsmall(wt, row_f, row_ab, nh):
    assert row_f % LANE == 0 and row_ab % LANE == nh and 3 * nh <= LANE
    k = wt.shape[1]
    tk = _pick(k, 2048, LANE)
    return pl.pallas_call(
        functools.partial(_wsmall_kernel, nh=nh),
        grid=(k // tk,),
        in_specs=[pl.BlockSpec((LANE, tk), lambda j: (row_f // LANE, j)),
                  pl.BlockSpec((LANE, tk), lambda j: (row_ab // LANE, j))],
        out_specs=pl.BlockSpec((LANE, tk), lambda j: (0, j)),
        out_shape=jax.ShapeDtypeStruct((LANE, k), BF16),
        compiler_params=_params("parallel"),
        name="wsmall",
    )(wt, wt)


def _mm_kernel(x_ref, w_ref, o_ref):
    o_ref[...] = lax.dot_general(x_ref[...], w_ref[...], NT, preferred_element_type=F32).astype(o_ref.dtype)


def _matmul(x, wt, tm, tn, out_dtype=F32):
    m, k = x.shape
    n = wt.shape[0]
    return pl.pallas_call(
        _mm_kernel,
        grid=(m // tm, n // tn),
        in_specs=[pl.BlockSpec((tm, k), lambda i, j: (i, 0)),
                  pl.BlockSpec((tn, k), lambda i, j: (j, 0))],
        out_specs=pl.BlockSpec((tm, tn), lambda i, j: (i, j)),
        out_shape=jax.ShapeDtypeStruct((m, n), out_dtype),
        compiler_params=_params("parallel", "arbitrary"),
        name="inproj",
    )(x, wt)


def _small_kernel(x_ref, w_ref, p_ref, o_ref, *, nh):
    acc = lax.dot_general(x_ref[...], w_ref[...], NT, preferred_element_type=F32)
    xb = acc + p_ref[0:1, :]
    lane = lax.broadcasted_iota(jnp.int32, acc.shape, 1)
    sp = _softplus(xb)
    logf = jnp.minimum(xb, 0.0) - jnp.log1p(jnp.exp(-jnp.abs(xb)))
    g = -jnp.exp(p_ref[1:2, :]) * sp
    beta = jax.nn.sigmoid(acc)
    o_ref[...] = jnp.where(lane < nh, logf, jnp.where(lane < 2 * nh, g, jnp.where(lane < 3 * nh, beta, 0.0)))


def _small_heads(h, w_small, p_small, nh, tm):
    m, k = h.shape
    return pl.pallas_call(
        functools.partial(_small_kernel, nh=nh),
        grid=(m // tm,),
        in_specs=[pl.BlockSpec((tm, k), lambda i: (i, 0)),
                  pl.BlockSpec((LANE, k), lambda i: (0, 0)),
                  pl.BlockSpec((SUBLANE, LANE), lambda i: (0, 0))],
        out_specs=pl.BlockSpec((tm, LANE), lambda i: (i, 0)),
        out_shape=jax.ShapeDtypeStruct((m, LANE), F32),
        compiler_params=_params("parallel"),
        name="small_heads",
    )(h, w_small, p_small)


def _qknorm_kernel(x_ref, qg_ref, kg_ref, qn_ref, kn_ref, vb_ref, vt_ref, kf_ref, vf_ref, *, nq, nkv):
    def norm(x, g):
        return x * lax.rsqrt(jnp.mean(x * x, axis=-1, keepdims=True) + RMS_EPS) * g

    for h in range(nq):
        sl = slice(h * LANE, (h + 1) * LANE)
        qn_ref[:, sl] = norm(x_ref[:, sl], qg_ref[...]).astype(BF16)
    for h in range(nkv):
        sl = slice(h * LANE, (h + 1) * LANE)
        kx = norm(x_ref[:, (nq + h) * LANE:(nq + h + 1) * LANE], kg_ref[...])
        kf_ref[:, sl] = kx
        kn_ref[:, sl] = kx.astype(BF16)
        vx = x_ref[:, (nq + nkv + h) * LANE:(nq + nkv + h + 1) * LANE]
        vf_ref[:, sl] = vx
        vb_ref[:, sl] = vx.astype(BF16)
        vt_ref[sl, :] = vx.T.astype(BF16)


def _qknorm(proj, q_gain, k_gain, nq, nkv, tm):
    m = proj.shape[0]
    wq, wk = nq * LANE, nkv * LANE
    row = lambda w: pl.BlockSpec((tm, w), lambda i: (i, 0))
    gain = pl.BlockSpec((1, LANE), lambda i: (0, 0))
    return pl.pallas_call(
        functools.partial(_qknorm_kernel, nq=nq, nkv=nkv),
        grid=(m // tm,),
        in_specs=[row(wq + 2 * wk), gain, gain],
        out_specs=[row(wq), row(wk), row(wk), pl.BlockSpec((wk, tm), lambda i: (0, i)), row(wk), row(wk)],
        out_shape=[jax.ShapeDtypeStruct((m, wq), BF16), jax.ShapeDtypeStruct((m, wk), BF16),
                   jax.ShapeDtypeStruct((m, wk), BF16), jax.ShapeDtypeStruct((wk, m), BF16),
                   jax.ShapeDtypeStruct((m, wk), F32), jax.ShapeDtypeStruct((m, wk), F32)],
        compiler_params=_params("parallel"),
        name="qknorm",
    )(proj, q_gain.reshape(1, LANE), k_gain.reshape(1, LANE))


def _cumf_kernel(x_ref, o_ref, *, blk):
    t = x_ref.shape[0]
    r = lax.broadcasted_iota(jnp.int32, (blk, blk), 0)
    c = lax.broadcasted_iota(jnp.int32, (blk, blk), 1)
    tri = (r >= c).astype(BF16)
    carry = jnp.zeros((1, LANE), F32)
    for i in range(t // blk):
        cum = _dot_sel(tri, x_ref[i * blk:(i + 1) * blk, :]) + carry
        carry = cum[blk - 1:blk, :]
        o_ref[i * blk:(i + 1) * blk, :] = cum


def _cumf(small, nb, t):
    blk = _pick(t, 256, LANE)
    return pl.pallas_call(
        functools.partial(_cumf_kernel, blk=blk),
        grid=(nb,),
        in_specs=[pl.BlockSpec((t, LANE), lambda b: (b, 0))],
        out_specs=pl.BlockSpec((t, LANE), lambda b: (b, 0)),
        out_shape=jax.ShapeDtypeStruct((nb * t, LANE), F32),
        compiler_params=_params("parallel"),
        name="cumf",
    )(small)


def _fox_prompt_kernel(q_ref, k_ref, vt_ref, f_ref, o_ref, m_sc, l_sc, acc_sc, *, g, tq, scale):
    qi = pl.program_id(2)
    m_sc[...] = jnp.full(m_sc.shape, NEG_INF, F32)
    l_sc[...] = jnp.zeros(l_sc.shape, F32)
    acc_sc[...] = jnp.zeros(acc_sc.shape, F32)
    qs = jnp.concatenate([q_ref[:, i * LANE:(i + 1) * LANE] for i in range(g)], axis=0)

    def block(ki, masked):
        ks = pl.ds(pl.multiple_of(ki * tq, tq), tq)
        st = lax.dot_general(k_ref[ks, :], qs, NT, preferred_element_type=F32) * scale
        st = st - jnp.concatenate([jnp.broadcast_to(f_ref[0, 0, ks, i:i + 1], (tq, tq)) for i in range(g)], axis=1)
        if masked:
            keep = lax.broadcasted_iota(jnp.int32, (tq, tq), 0) <= lax.broadcasted_iota(jnp.int32, (tq, tq), 1)
            st = jnp.where(jnp.concatenate([keep] * g, axis=1), st, NEG_INF)
        m_prev = m_sc[...]
        m_new = jnp.maximum(m_prev, jnp.max(st, axis=0, keepdims=True))
        alpha = jnp.exp(m_prev - m_new)
        p = jnp.exp(st - m_new)
        l_sc[...] = alpha * l_sc[...] + jnp.sum(p, axis=0, keepdims=True)
        acc_sc[...] = alpha * acc_sc[...] + jnp.dot(vt_ref[:, ks], p.astype(BF16), preferred_element_type=F32)
        m_sc[...] = m_new

    def body(ki, carry):
        block(ki, False)
        return carry

    lax.fori_loop(0, qi, body, 0)
    block(qi, True)
    ot = acc_sc[...] / l_sc[...]
    for i in range(g):
        o_ref[:, i * LANE:(i + 1) * LANE] = ot[:, i * tq:(i + 1) * tq].T.astype(o_ref.dtype)


def _fox_prompt(qn, kn, vt, fcol, nb, t, nkv, g):
    tq = _pick(t, 512, LANE)
    nq = t // tq
    n = g * tq
    return pl.pallas_call(
        functools.partial(_fox_prompt_kernel, g=g, tq=tq, scale=LANE ** -0.5),
        grid=(nb, nkv, nq),
        in_specs=[pl.BlockSpec((tq, g * LANE), lambda b, h, qi: (b * nq + qi, h)),
                  pl.BlockSpec((t, LANE), lambda b, h, qi: (b, h)),
                  pl.BlockSpec((LANE, t), lambda b, h, qi: (h, b)),
                  pl.BlockSpec((1, 1, t, LANE), lambda b, h, qi: (b, h, 0, 0))],
        out_specs=pl.BlockSpec((tq, g * LANE), lambda b, h, qi: (b * nq + qi, h)),
        out_shape=jax.ShapeDtypeStruct((nb * t, nkv * g * LANE), BF16),
        scratch_shapes=[pltpu.VMEM((1, n), F32), pltpu.VMEM((1, n), F32), pltpu.VMEM((LANE, n), F32)],
        compiler_params=_params("parallel", "parallel", "arbitrary"),
        name="fox_prompt",
    )(qn, kn, vt, fcol)


def _fox_sample_kernel(pt_ref, q_ref, kn_ref, vn_ref, lfn_ref, *refs, npg, nkv, g, ts, scale):
    k_refs, v_refs, lf_refs = refs[:npg], refs[npg:2 * npg], refs[2 * npg:3 * npg]
    o_ref, m_sc, l_sc, acc_sc, f_sc = refs[3 * npg:]
    p = pl.program_id(1)
    page = k_refs[0].shape[1]
    nh = nkv * g
    rows = ts * nh
    cols = page * nkv

    @pl.when(p == 0)
    def _():
        m_sc[...] = jnp.full(m_sc.shape, NEG_INF, F32)
        l_sc[...] = jnp.zeros(l_sc.shape, F32)
        acc_sc[...] = jnp.zeros(acc_sc.shape, F32)
        f_sc[...] = jnp.zeros(f_sc.shape, F32)

    q = q_ref[0]

    def update(s, pv):
        m_prev = m_sc[...]
        m_new = jnp.maximum(m_prev, jnp.max(s, axis=-1, keepdims=True))
        alpha = jnp.exp(m_prev - m_new)
        pr = jnp.exp(s - m_new)
        l_sc[...] = alpha * l_sc[...] + jnp.sum(pr, axis=-1, keepdims=True)
        acc_sc[...] = alpha * acc_sc[...] + pv(pr.astype(BF16))
        m_sc[...] = m_new

    def own_kv(width):
        rq = lax.broadcasted_iota(jnp.int32, (rows, width), 0)
        cq = lax.broadcasted_iota(jnp.int32, (rows, width), 1)
        return rq, cq, (cq % nkv) == ((rq % nh) // g)

    ue = (lax.broadcasted_iota(jnp.int32, (page, cols), 0)
          <= lax.broadcasted_iota(jnp.int32, (page, cols), 1) // nkv).astype(BF16)
    _, _, own = own_kv(cols)
    carry = f_sc[...]
    parts = []
    for i in range(npg):
        fexp = _dot_sel(ue, lf_refs[i][0], sel_first=False) + carry[:, 0:1]
        carry = jnp.broadcast_to(fexp[:, cols - 1:cols], carry.shape)
        kf = k_refs[i][0].reshape(cols, LANE).astype(BF16)
        s = lax.dot_general(q, kf, NT, preferred_element_type=F32) * scale - jnp.concatenate([fexp] * ts, axis=0)
        parts.append(jnp.where(own, s, NEG_INF))
    f_sc[...] = carry

    def pv_pages(pr):
        out = None
        for i in range(npg):
            t_ = jnp.dot(pr[:, i * cols:(i + 1) * cols], v_refs[i][0].reshape(cols, LANE).astype(BF16),
                         preferred_element_type=F32)
            out = t_ if out is None else out + t_
        return out

    update(jnp.concatenate(parts, axis=1), pv_pages)

    @pl.when(p == pl.num_programs(1) - 1)
    def _():
        nn = kn_ref.shape[1]
        nl = lfn_ref.shape[1]
        uen = (lax.broadcasted_iota(jnp.int32, (nl, nn), 0)
               <= lax.broadcasted_iota(jnp.int32, (nl, nn), 1) // nkv).astype(BF16)
        fnew = _dot_sel(uen, lfn_ref[0], sel_first=False, dims=TN)[:nh] + f_sc[...][:, 0:1]
        rq, cq, own_n = own_kv(nn)
        keep = own_n & ((cq // nkv) <= (rq // nh)) & (cq < ts * nkv)
        s = lax.dot_general(q, kn_ref[0], NT, preferred_element_type=F32) * scale - jnp.concatenate([fnew] * ts, axis=0)
        update(jnp.where(keep, s, NEG_INF), lambda pr: jnp.dot(pr, vn_ref[0], preferred_element_type=F32))
        o_ref[0] = (acc_sc[...] / l_sc[...]).astype(o_ref.dtype)


def _fox_sample(page_ids, qb, knew, vnew, lfnew, kc, vc, lfc, nkv, g, ts):
    bs, rows, _ = qb.shape
    npages = page_ids.shape[1]
    npg = _pick(npages, 16, 1)
    page = kc.shape[1]
    nh = lfc.shape[1]
    nn = knew.shape[1]
    nl = lfnew.shape[1]

    def kv_page(i):
        return pl.BlockSpec((1, page, nkv, LANE), lambda b, p, pt, i=i: (pt[b * npages + p * npg + i], 0, 0, 0))

    def lf_page(i):
        return pl.BlockSpec((1, nh, page), lambda b, p, pt, i=i: (pt[b * npages + p * npg + i], 0, 0))

    per_b = lambda r: pl.BlockSpec((1, r, LANE), lambda b, p, pt: (b, 0, 0))
    in_specs = [per_b(rows), per_b(nn), per_b(nn), per_b(nl)]
    in_specs += [kv_page(i) for i in range(npg)] + [kv_page(i) for i in range(npg)] + [lf_page(i) for i in range(npg)]
    grid_spec = pltpu.PrefetchScalarGridSpec(
        num_scalar_prefetch=1,
        grid=(bs, npages // npg),
        in_specs=in_specs,
        out_specs=per_b(rows),
        scratch_shapes=[pltpu.VMEM((rows, 1), F32), pltpu.VMEM((rows, 1), F32), pltpu.VMEM((rows, LANE), F32),
                        pltpu.VMEM((nh, LANE), F32)],
    )
    return pl.pallas_call(
        functools.partial(_fox_sample_kernel, npg=npg, nkv=nkv, g=g, ts=ts, scale=LANE ** -0.5),
        grid_spec=grid_spec,
        out_shape=jax.ShapeDtypeStruct((bs, rows, LANE), BF16),
        compiler_params=_params("parallel", "arbitrary"),
        name="fox_sample",
    )(page_ids.reshape(-1), qb, knew, vnew, lfnew, *([kc] * npg), *([vc] * npg), *([lfc] * npg))


def _gdn_kernel(*refs, hb, c, taps, use_state, scale):
    if use_state:
        (q_ref, k_ref, v_ref, wq_ref, wk_ref, wv_ref, z_ref, gb_ref, nw_ref, hq_ref, hk_ref, hv_ref, s0_ref,
         o_ref, so_ref, s_sc, tail_sc) = refs
    else:
        q_ref, k_ref, v_ref, wq_ref, wk_ref, wv_ref, z_ref, gb_ref, nw_ref, o_ref, so_ref, s_sc, tail_sc = refs
    n = pl.program_id(2)

    @pl.when(n == 0)
    def _():
        s_sc[...] = s0_ref[0] if use_state else jnp.zeros(s_sc.shape, F32)
        if use_state:
            for i, h_ref in enumerate((hq_ref, hk_ref, hv_ref)):
                tail_sc[i] = h_ref[...]
        else:
            tail_sc[...] = jnp.zeros(tail_sc.shape, F32)

    heads = range(hb)

    def conv_act(x_ref, w_ref, idx, l2_scale):
        x = x_ref[...]
        ext = jnp.concatenate([tail_sc[idx], x], axis=0)
        tail_sc[idx] = x[c - SUBLANE:c]
        y = x * w_ref[taps - 1:taps, :]
        for back in range(1, taps):
            y = y + pltpu.roll(ext, back, 0)[SUBLANE:SUBLANE + c] * w_ref[taps - 1 - back:taps - back, :]
        y = _silu(y)
        cols = [y[:, h * LANE:(h + 1) * LANE] for h in heads]
        if l2_scale is not None:
            cols = [yh * (lax.rsqrt(jnp.sum(yh * yh, axis=-1, keepdims=True) + L2_EPS) * l2_scale) for yh in cols]
        return jnp.stack(cols)

    q = conv_act(q_ref, wq_ref, 0, scale)
    k = conv_act(k_ref, wk_ref, 1, 1.0)
    v = conv_act(v_ref, wv_ref, 2, None)
    gb = gb_ref[0]
    r = lax.broadcasted_iota(jnp.int32, (c, c), 0)
    cc = lax.broadcasted_iota(jnp.int32, (c, c), 1)
    incl, strict = r >= cc, r > cc
    gc_all = _dot_sel(incl.astype(BF16), gb)
    gc_t = gc_all.T
    gcol = jnp.stack([gc_all[:, h:h + 1] for h in heads])
    grow = jnp.stack([gc_t[h:h + 1, :] for h in heads])
    beta = jnp.stack([gb[:, hb + h:hb + h + 1] for h in heads])
    glast = gcol[:, c - 1:c, :]
    decay = jnp.exp(jnp.where(incl[None], gcol - grow, NEG_INF))
    kb = k * beta
    bdot = lambda a, b, eq: jnp.einsum(eq, a.astype(BF16), b.astype(BF16), preferred_element_type=F32)

    mm = jnp.where(strict[None], bdot(kb, k, 'hck,hsk->hcs') * decay, 0.0)
    eye = (r == cc).astype(F32)[None]
    base = min(c, SUBLANE)
    same = lambda s: ((r // s) == (cc // s))[None]
    mp = jnp.where(same(base), mm, 0.0)
    tinv = eye - mp
    span = 2
    while span < base:
        mp = bdot(mp, mp, 'hcs,hsk->hck')
        tinv = bdot(tinv, eye + mp, 'hcs,hsk->hck')
        span *= 2
    blk = base
    while blk < c:
        off = jnp.where(same(2 * blk) & ~same(blk), mm, 0.0)
        tinv = tinv - bdot(tinv, bdot(off, tinv, 'hcs,hsk->hck'), 'hcs,hsk->hck')
        blk *= 2
    egc = jnp.exp(gcol)
    u = bdot(tinv, v * beta, 'hcs,hsv->hcv')
    w = bdot(tinv, kb * egc, 'hcs,hsk->hck')
    a = bdot(q, k, 'hck,hsk->hcs') * decay
    qd = q * egc
    kd = k * jnp.exp(glast - gcol)
    s = s_sc[...]
    v_new = u - bdot(w, s, 'hck,hkv->hcv')
    o = bdot(qd, s, 'hck,hkv->hcv') + bdot(a, v_new, 'hcs,hsv->hcv')
    s_new = s * jnp.exp(glast) + bdot(kd, v_new, 'hck,hcv->hkv')
    s_sc[...] = s_new

    @pl.when(n == pl.num_programs(2) - 1)
    def _():
        so_ref[0] = s_new

    on = o * lax.rsqrt(jnp.mean(o * o, axis=-1, keepdims=True) + RMS_EPS) * nw_ref[...][None]
    for h in heads:
        sl = slice(h * LANE, (h + 1) * LANE)
        o_ref[:, sl] = (on[h] * _silu(z_ref[:, sl])).astype(o_ref.dtype)


def _gdn(x, conv_w, gb, norm_w, hist, s0, nb, tseq, c, hb, nheads):
    m = x.shape[0]
    n = tseq // c
    w = hb * LANE
    ng = nheads // hb
    kw = nheads * LANE
    taps = conv_w.shape[0]
    row_col = lambda cb: pl.BlockSpec((c, w), lambda b, hg, i, cb=cb: (b * n + i, cb + hg))
    wcol = lambda cb: pl.BlockSpec((taps, w), lambda b, hg, i, cb=cb: (0, cb + hg))
    in_specs = [row_col(0), row_col(kw // w), row_col(2 * kw // w),
                wcol(0), wcol(kw // w), wcol(2 * kw // w),
                row_col(3 * kw // w),
                pl.BlockSpec((1, c, LANE), lambda b, hg, i: (hg, b * n + i, 0)),
                pl.BlockSpec((1, LANE), lambda b, hg, i: (0, 0))]
    args = [x, x, x, conv_w, conv_w, conv_w, x, gb, norm_w.reshape(1, LANE)]
    state_spec = pl.BlockSpec((1, hb, LANE, LANE), lambda b, hg, i: (b, hg, 0, 0))
    if s0 is not None:
        hcol = lambda cb: pl.BlockSpec((SUBLANE, w), lambda b, hg, i, cb=cb: (b, cb + hg))
        in_specs += [hcol(0), hcol(kw // w), hcol(2 * kw // w), state_spec]
        args += [hist, hist, hist, s0]
    return pl.pallas_call(
        functools.partial(_gdn_kernel, hb=hb, c=c, taps=taps, use_state=s0 is not None, scale=LANE ** -0.5),
        grid=(nb, ng, n),
        in_specs=in_specs,
        out_specs=[pl.BlockSpec((c, w), lambda b, hg, i: (b * n + i, hg)), state_spec],
        out_shape=[jax.ShapeDtypeStruct((m, kw), BF16), jax.ShapeDtypeStruct((nb, nheads, LANE, LANE), F32)],
        scratch_shapes=[pltpu.VMEM((hb, LANE, LANE), F32), pltpu.VMEM((3, SUBLANE, w), F32)],
        compiler_params=_params("parallel", "parallel", "arbitrary"),
        name="gdn",
    )(*args)


def _merge_kernel(of_ref, og_ref, wf_ref, wg_ref, gf_ref, gg_ref, o_ref):
    a = jnp.dot(of_ref[...], wf_ref[...].astype(BF16), preferred_element_type=F32)
    b = jnp.dot(og_ref[...], wg_ref[...].astype(BF16), preferred_element_type=F32)
    o_ref[...] = (jax.nn.sigmoid(gf_ref[...]) * a + jax.nn.sigmoid(gg_ref[...]) * b).astype(o_ref.dtype)


def _merge(o_fox, o_gdn, w_bf, w_bg, proj, gf_col0, gg_col0, tm, tn):
    m, kf = o_fox.shape
    kg = o_gdn.shape[1]
    n = w_bf.shape[1]
    return pl.pallas_call(
        _merge_kernel,
        grid=(m // tm, n // tn),
        in_specs=[pl.BlockSpec((tm, kf), lambda i, j: (i, 0)),
                  pl.BlockSpec((tm, kg), lambda i, j: (i, 0)),
                  pl.BlockSpec((kf, tn), lambda i, j: (0, j)),
                  pl.BlockSpec((kg, tn), lambda i, j: (0, j)),
                  pl.BlockSpec((tm, tn), lambda i, j: (i, gf_col0 // tn + j)),
                  pl.BlockSpec((tm, tn), lambda i, j: (i, gg_col0 // tn + j))],
        out_specs=pl.BlockSpec((tm, tn), lambda i, j: (i, j)),
        out_shape=jax.ShapeDtypeStruct((m, n), BF16),
        compiler_params=_params("parallel", "arbitrary"),
        name="merge",
    )(o_fox, o_gdn, w_bf, w_bg, proj, proj)


def _resid_kernel(a_ref, w_ref, x_ref, gt_ref, o_ref):
    acc = jnp.dot(a_ref[...], w_ref[...].astype(BF16), preferred_element_type=F32)
    o_ref[0] = x_ref[0] + _rows(acc, gt_ref[0]) * acc


def _resid_matmul(a, w, x3, gt3, tiles_per_mod, tn, name):
    nt, tm, n = x3.shape
    k = a.shape[1]
    mr = gt3.shape[1]
    return pl.pallas_call(
        _resid_kernel,
        grid=(nt, n // tn),
        in_specs=[pl.BlockSpec((tm, k), lambda i, j: (i, 0)),
                  pl.BlockSpec((k, tn), lambda i, j: (0, j)),
                  pl.BlockSpec((1, tm, tn), lambda i, j: (i, 0, j)),
                  pl.BlockSpec((1, mr, tn), lambda i, j: (i // tiles_per_mod, 0, j))],
        out_specs=pl.BlockSpec((1, tm, tn), lambda i, j: (i, 0, j)),
        out_shape=jax.ShapeDtypeStruct((nt, tm, n), F32),
        compiler_params=_params("parallel", "arbitrary"),
        name=name,
    )(a, w, x3, gt3)


def _ffn_up_kernel(*refs, stride, taps, tiles_per_seq, halo_from_h):
    if halo_from_h:
        hh_ref, h_ref, wg_ref, wu_ref, cw_ref, act_ref, tail_ref, hcat = refs
    else:
        hist_ref, h_ref, wg_ref, wu_ref, cw_ref, act_ref, tail_ref = refs
    i, j = pl.program_id(0), pl.program_id(1)
    tm = h_ref.shape[0]
    if halo_from_h:
        hh = hh_ref.shape[0]

        @pl.when(j == 0)
        def _():
            hcat[0:hh, :] = jnp.where(i % tiles_per_seq == 0, jnp.zeros_like(hh_ref[...]), hh_ref[...])
            hcat[hh:hh + tm, :] = h_ref[...]

        ext = jnp.dot(hcat[...], wg_ref[...].astype(BF16), preferred_element_type=F32)
    else:
        hh = hist_ref.shape[0]
        ext = jnp.concatenate(
            [hist_ref[...], jnp.dot(h_ref[...], wg_ref[...].astype(BF16), preferred_element_type=F32)], axis=0)
    up = jnp.dot(h_ref[...], wu_ref[...].astype(BF16), preferred_element_type=F32)
    y = None
    for tap in range(taps):
        off = hh - (taps - 1 - tap) * stride
        term = ext[off:off + tm] * cw_ref[tap:tap + 1, :]
        y = term if y is None else y + term
    act_ref[...] = (_silu(y) * up).astype(act_ref.dtype)
    tail_ref[0] = ext[hh + tm - tail_ref.shape[1]:hh + tm]


def _ffn_up(h, hist, w_gate, w_up, conv_w, tm, tn, stride, tiles_per_seq, tail_rows):
    m, k = h.shape
    n = w_gate.shape[1]
    taps = conv_w.shape[0]
    halo_from_h = hist is None
    if halo_from_h:
        hh = BF16_ROWS
        first = pl.BlockSpec((hh, k), lambda i, j: (jnp.maximum(i * (tm // hh) - 1, 0), 0))
        first_arg = h
        scratch = [pltpu.VMEM((hh + tm, k), BF16)]
    else:
        hh = hist.shape[0]
        first = pl.BlockSpec((hh, tn), lambda i, j: (0, j))
        first_arg = hist
        scratch = []
    return pl.pallas_call(
        functools.partial(_ffn_up_kernel, stride=stride, taps=taps, tiles_per_seq=tiles_per_seq,
                          halo_from_h=halo_from_h),
        grid=(m // tm, n // tn),
        in_specs=[first,
                  pl.BlockSpec((tm, k), lambda i, j: (i, 0)),
                  pl.BlockSpec((k, tn), lambda i, j: (0, j)),
                  pl.BlockSpec((k, tn), lambda i, j: (0, j)),
                  pl.BlockSpec((taps, tn), lambda i, j: (0, j))],
        out_specs=[pl.BlockSpec((tm, tn), lambda i, j: (i, j)),
                   pl.BlockSpec((1, tail_rows, tn), lambda i, j: (i, 0, j))],
        out_shape=[jax.ShapeDtypeStruct((m, n), BF16), jax.ShapeDtypeStruct((m // tm, tail_rows, n), F32)],
        scratch_shapes=scratch,
        compiler_params=_params("parallel", "arbitrary"),
        name="ffn_up",
    )(first_arg, h, w_gate, w_up, conv_w)


def _layer(x2, mods, wts, lay, *, prompt, nb, tseq, fox_fn, gdn_s0, gdn_hist, ffn_hist):
    m, d = x2.shape
    sh_m, sc_m, gt_m, sh_f, sc_f, gt_f = mods
    nh, nkv, g, gh = lay["nh"], lay["nkv"], lay["g"], lay["gh"]
    kw = gh * LANE
    if prompt:
        tr = _pick(tseq, 256, SUBLANE)
        tm = _pick(tseq, 1024, LANE)
        stride = 1
    else:
        tr = nb
        tm = m
        stride = nb
    tiles_mod = (tseq // tr) if prompt else m // tr
    h = _normmod(x2.reshape(m // tr, tr, d), wts["norm_mix"], sc_m, sh_m, tiles_mod).reshape(m, d)
    mm = lambda wt: _matmul(h, wt, tm, _pick(wt.shape[0], 1024, LANE))
    proj_fox, proj_gdn, proj_gate = mm(wts["w_fox"]), mm(wts["w_gdn"]), mm(wts["w_gate"])
    small = _small_heads(h, wts["w_small"], wts["p_small"], nh, tm)
    qn, kn, vb, vt, k_f, v_f = _qknorm(proj_fox, wts["fox_q_norm"], wts["fox_k_norm"], nh, nkv, _pick(m, 256, LANE))
    o_fox = fox_fn(qn, kn, vb, vt, small)
    hb = lay["hb"]
    if prompt:
        o_gdn, s_fin = _gdn(proj_gdn, wts["gdn_conv_w"], _group_gb(small, nh, gh, hb), wts["gdn_norm"], None, None,
                            nb, tseq, lay["chunk"], hb, gh)
    else:
        ts = m // nb
        cpad = lay["cpad"]
        to_bm = lambda a: jnp.pad(a.reshape(ts, nb, -1).transpose(1, 0, 2),
                                  ((0, 0), (0, cpad - ts), (0, 0))).reshape(nb * cpad, -1)
        o_bm, s_fin = _gdn(to_bm(proj_gdn), wts["gdn_conv_w"], _group_gb(to_bm(small), nh, gh, hb), wts["gdn_norm"],
                           gdn_hist, gdn_s0, nb, cpad, cpad, hb, gh)
        o_gdn = o_bm.reshape(nb, cpad, kw)[:, :ts].transpose(1, 0, 2).reshape(m, kw)
    tn = _pick(d, 512, LANE)
    merged = _merge(o_fox, o_gdn, wts["w_branch_fox"], wts["w_branch_gdn"], proj_gate, 0, d, tm, tn)
    tiles_mod_m = (tseq // tm) if prompt else 1
    x3 = _resid_matmul(merged, wts["w_out"], x2.reshape(m // tm, tm, d), gt_m, tiles_mod_m, tn, "out_proj")
    x2 = x3.reshape(m, d)
    h2 = _normmod(x2.reshape(m // tr, tr, d), wts["norm_ffn"], sc_f, sh_f, tiles_mod).reshape(m, d)
    dff = wts["ffn_w_gate"].shape[1]
    tnf = _pick(dff, 256, LANE)
    act, tail = _ffn_up(h2, ffn_hist, wts["ffn_w_gate"], wts["ffn_w_up"], wts["ffn_conv_w"], tm, tnf, stride,
                        tseq // tm if prompt else 1, SUBLANE if prompt else m)
    tm2 = _pick(tm, 512, LANE) if prompt else m
    y3 = _resid_matmul(act, wts["ffn_w_down"], x2.reshape(m // tm2, tm2, d), gt_f,
                       (tseq // tm2) if prompt else 1, _pick(d, 512, LANE), "ffn_down")
    return y3.reshape(m, d), dict(gdn=proj_gdn, small=small, k=k_f, v=v_f, s=s_fin, tail=tail)


def _group_gb(small, nh, gh, hb):
    rows = small.shape[0]
    gg = small[:, nh:nh + gh].reshape(rows, gh // hb, hb)
    bb = small[:, nh + gh:nh + 2 * gh].reshape(rows, gh // hb, hb)
    gb = jnp.concatenate([gg, bb], axis=-1).transpose(1, 0, 2)
    return jnp.pad(gb, ((0, 0), (0, 0), (0, LANE - 2 * hb)))


def kernel(x_prompt, x_sample, cache_k, cache_v, cache_logf, state_gdn, state_gdn_conv, state_ffn_conv, page_table, c_prompt, c_sample, w_ada, b_ada, norm_mix, norm_ffn, w_in, fox_b_f, fox_q_norm, fox_k_norm, gdn_conv_w, gdn_A_log, gdn_dt_bias, gdn_norm, w_branch_fox, w_branch_gdn, w_out, ffn_w_gate, ffn_w_up, ffn_conv_w, ffn_w_down):
    nb, t, d = x_prompt.shape
    bs, ts, _ = x_sample.shape
    depth = w_in.shape[0]
    page, nkv = cache_k.shape[2], cache_k.shape[3]
    nh = cache_logf.shape[-1]
    g = nh // nkv
    gh = state_gdn.shape[2]
    kw = gh * LANE
    gconv = gdn_conv_w.shape[1]
    fconv = ffn_conv_w.shape[1]
    dff = ffn_w_gate.shape[-1]
    assert cache_k.shape[-1] == LANE and state_gdn.shape[-1] == LANE and state_gdn.shape[-2] == LANE
    assert gh == nh and ts >= gconv - 1 and ts >= fconv - 1

    sizes = (nh * LANE, nkv * LANE, nkv * LANE, nh, 3 * kw, kw, gh, gh, d, d)
    starts = [0]
    for s_ in sizes:
        starts.append(starts[-1] + s_)
    assert starts[-1] == w_in.shape[-1]
    hb = _pick(gh, 16, 1)
    lay = dict(nh=nh, nkv=nkv, g=g, gh=gh, hb=hb, chunk=_pick(t, 64, SUBLANE), cpad=SUBLANE)

    xs = x_sample.transpose(1, 0, 2).reshape(ts * bs, d)
    xp = x_prompt.reshape(nb * t, d)
    mc = -(-nb // SUBLANE) * SUBLANE
    c_all = jnp.concatenate([c_prompt, jnp.zeros((mc - nb, d), F32), c_sample], axis=0)

    outs_p = [[] for _ in range(6)]
    outs_s = [[] for _ in range(6)]
    for l in range(depth):
        nin = w_in.shape[2]
        wt = jnp.swapaxes(w_in, 1, 2).reshape(depth * nin, w_in.shape[1])
        run = lambda a, b: _wprep(wt, l * nin + starts[a], starts[b] - starts[a])
        w_small = _wsmall(wt, l * nin + starts[3], l * nin + starts[6], nh)
        zpad = jnp.zeros((LANE - 2 * nh,), F32)
        p_small = jnp.stack([jnp.concatenate([fox_b_f[l], gdn_dt_bias[l], zpad]),
                             jnp.concatenate([jnp.zeros((nh,), F32), gdn_A_log[l], zpad])])
        p_small = jnp.pad(p_small, ((0, SUBLANE - 2), (0, 0)))
        wts = dict(norm_mix=norm_mix[l], norm_ffn=norm_ffn[l], w_fox=run(0, 3), w_gdn=run(4, 6), w_gate=run(8, 10),
                   w_small=w_small, p_small=p_small,
                   fox_q_norm=fox_q_norm[l], fox_k_norm=fox_k_norm[l], gdn_conv_w=gdn_conv_w[l],
                   gdn_norm=gdn_norm[l], w_branch_fox=w_branch_fox[l], w_branch_gdn=w_branch_gdn[l],
                   w_out=w_out[l], ffn_w_gate=ffn_w_gate[l], ffn_w_up=ffn_w_up[l],
                   ffn_conv_w=ffn_conv_w[l], ffn_w_down=ffn_w_down[l].astype(BF16))

        mod = _ada(c_all, w_ada[l], b_ada[l])
        mods_p = [mod[:nb, i * d:(i + 1) * d].reshape(nb, 1, d) for i in range(6)]
        mods_s = [mod[mc:, i * d:(i + 1) * d].reshape(1, bs, d) for i in range(6)]

        def fox_p(qn, kn, vb, vt, small):
            f_tm = _cumf(small, nb, t)
            fcol = f_tm[:, :nh].reshape(nb, t, nkv, g).transpose(0, 2, 1, 3)
            fcol = jnp.pad(fcol, ((0, 0), (0, 0), (0, 0), (0, LANE - g)))
            return _fox_prompt(qn, kn, vt, fcol, nb, t, nkv, g)

        xp, st = _layer(xp, mods_p, wts, lay, prompt=True, nb=nb, tseq=t, fox_fn=fox_p, gdn_s0=None,
                        gdn_hist=None, ffn_hist=None)
        outs_p[0].append(st["k"].reshape(nb, t, nkv, LANE))
        outs_p[1].append(st["v"].reshape(nb, t, nkv, LANE))
        outs_p[2].append(st["small"].reshape(nb, t, LANE)[:, :, :nh])
        outs_p[3].append(st["s"])
        outs_p[4].append(st["gdn"].reshape(nb, t, -1)[:, t - (gconv - 1):, :3 * kw])
        tail = st["tail"].reshape(nb, -1, SUBLANE, dff)
        outs_p[5].append(tail[:, -1, SUBLANE - (fconv - 1):, :])

        n_pool = cache_k.shape[1]
        kc = cache_k.reshape(depth * n_pool, page, nkv, LANE)
        vc = cache_v.reshape(depth * n_pool, page, nkv, LANE)
        lfc = jnp.swapaxes(cache_logf, 2, 3).reshape(depth * n_pool, nh, page)
        page_ids = page_table + l * n_pool

        def fox_s(qn, kn, vb, vt, small):
            bm = lambda a, n: a.reshape(ts, bs, n, LANE).transpose(1, 0, 2, 3).reshape(bs, ts * n, LANE)
            nn = -(-ts * nkv // BF16_ROWS) * BF16_ROWS
            padr = lambda a, n: jnp.pad(a, ((0, 0), (0, n - a.shape[1]), (0, 0)))
            lfn = padr(small.reshape(ts, bs, LANE).transpose(1, 0, 2), BF16_ROWS)
            o = _fox_sample(page_ids, bm(qn, nh), padr(bm(kn, nkv), nn), padr(bm(vb, nkv), nn), lfn, kc, vc, lfc,
                            nkv, g, ts)
            return o.reshape(bs, ts, nh, LANE).transpose(1, 0, 2, 3).reshape(ts * bs, nh * LANE)

        ghist = jnp.pad(state_gdn_conv[l], ((0, 0), (SUBLANE - (gconv - 1), 0), (0, 0))).reshape(bs * SUBLANE, 3 * kw)
        fhist = state_ffn_conv[l].transpose(1, 0, 2).reshape((fconv - 1) * bs, dff)
        xs, st = _layer(xs, mods_s, wts, lay, prompt=False, nb=bs, tseq=ts, fox_fn=fox_s, gdn_s0=state_gdn[l],
                        gdn_hist=ghist, ffn_hist=fhist)
        bm3 = lambda a: a.reshape(ts, bs, -1).transpose(1, 0, 2)
        outs_s[0].append(bm3(st["k"]).reshape(bs, ts, nkv, LANE))
        outs_s[1].append(bm3(st["v"]).reshape(bs, ts, nkv, LANE))
        outs_s[2].append(bm3(st["small"])[:, :, :nh])
        outs_s[3].append(st["s"])
        graw = bm3(st["gdn"])[:, :, :3 * kw]
        outs_s[4].append(jnp.concatenate([state_gdn_conv[l], graw], axis=1)[:, -(gconv - 1):])
        fraw = bm3(st["tail"].reshape(ts * bs, dff))
        outs_s[5].append(jnp.concatenate([state_ffn_conv[l], fraw], axis=1)[:, -(fconv - 1):])

    y_prompt = xp.reshape(nb, t, d)
    y_sample = xs.reshape(ts, bs, d).transpose(1, 0, 2)
    return (y_prompt, y_sample, *(jnp.stack(a) for a in outs_p), *(jnp.stack(a) for a in outs_s))
```

```python
import functools

import jax
import jax.numpy as jnp
from jax import lax
from jax.experimental import pallas as pl
from jax.experimental.pallas import tpu as pltpu

F32 = jnp.float32
BF16 = jnp.bfloat16
RMS_EPS = 1e-6
L2_EPS = 1e-6
NEG_INF = -1e30
LANE = 128
SUBLANE = 8
BF16_ROWS = 16
VMEM_LIMIT_BYTES = 56 * 2 ** 20

NN = (((1,), (0,)), ((), ()))
NT = (((1,), (1,)), ((), ()))
TN = (((0,), (0,)), ((), ()))


def _pick(n, pref, align):
    t = min(pref, n) // align * align
    while t >= align:
        if n % t == 0:
            return t
        t -= align
    return n


def _params(*sem):
    return pltpu.CompilerParams(dimension_semantics=sem, vmem_limit_bytes=VMEM_LIMIT_BYTES)


def _dot(a, b, dims=NN):
    return lax.dot_general(a.astype(BF16), b.astype(BF16), dims, preferred_element_type=F32)


def _dot_sel(sel, x, sel_first=True, dims=NN):
    h = x.astype(BF16)
    r = x - h.astype(F32)
    m = r.astype(BF16)
    l = (r - m.astype(F32)).astype(BF16)
    if sel_first:
        d = lambda p: lax.dot_general(sel, p, dims, preferred_element_type=F32)
    else:
        d = lambda p: lax.dot_general(p, sel, dims, preferred_element_type=F32)
    return d(h) + (d(m) + d(l))


def _silu(x):
    return x * jax.nn.sigmoid(x)


def _softplus(x):
    return jnp.maximum(x, 0.0) + jnp.log1p(jnp.exp(-jnp.abs(x)))


def _rows(a, m):
    if m.shape[0] == 1 or m.shape[0] == a.shape[0]:
        return m
    reps = a.shape[0] // m.shape[0]
    return jnp.concatenate([m] * reps, axis=0)


def _ada_kernel(c_ref, w_ref, b_ref, o_ref):
    a = _silu(c_ref[...])
    o_ref[...] = _dot(a, w_ref[...]) + b_ref[...]


def _ada(c_all, w_ada, b_ada):
    mc, d = c_all.shape
    n = w_ada.shape[1]
    tn = _pick(n, 512, LANE)
    return pl.pallas_call(
        _ada_kernel,
        grid=(n // tn,),
        in_specs=[pl.BlockSpec((mc, d), lambda j: (0, 0)),
                  pl.BlockSpec((d, tn), lambda j: (0, j)),
                  pl.BlockSpec((1, tn), lambda j: (0, j))],
        out_specs=pl.BlockSpec((mc, tn), lambda j: (0, j)),
        out_shape=jax.ShapeDtypeStruct((mc, n), F32),
        compiler_params=_params("parallel"),
        name="ada",
    )(c_all, w_ada, b_ada.reshape(1, n))


def _normmod_kernel(x_ref, g_ref, sc_ref, sh_ref, o_ref):
    x = x_ref[0]
    y = x * lax.rsqrt(jnp.mean(x * x, axis=-1, keepdims=True) + RMS_EPS) * g_ref[...]
    o_ref[0] = (y * (1.0 + _rows(y, sc_ref[0])) + _rows(y, sh_ref[0])).astype(o_ref.dtype)


def _normmod(x3, gain, sc3, sh3, tiles_per_mod):
    nt, tr, d = x3.shape
    mr = sc3.shape[1]
    mod_spec = pl.BlockSpec((1, mr, d), lambda i: (i // tiles_per_mod, 0, 0))
    return pl.pallas_call(
        _normmod_kernel,
        grid=(nt,),
        in_specs=[pl.BlockSpec((1, tr, d), lambda i: (i, 0, 0)),
                  pl.BlockSpec((1, d), lambda i: (0, 0)),
                  mod_spec, mod_spec],
        out_specs=pl.BlockSpec((1, tr, d), lambda i: (i, 0, 0)),
        out_shape=jax.ShapeDtypeStruct((nt, tr, d), BF16),
        compiler_params=_params("parallel"),
        name="normmod",
    )(x3, gain.reshape(1, d), sc3, sh3)


def _wprep_kernel(a_ref, b_ref, o_ref, *, delta):
    a = a_ref[...]
    if delta:
        a = jnp.concatenate([a[delta:], b_ref[0:delta, :]], axis=0)
    o_ref[...] = a.astype(BF16)


def _wprep(wt, row0, rows):
    k = wt.shape[1]
    tr = _pick(rows, 512, LANE)
    tk = _pick(k, 2048, LANE)
    delta, rb = row0 % tr, row0 // tr
    assert delta < LANE
    nxt = (lambda i, j: ((rb + i + 1) * (tr // LANE), j)) if delta else (lambda i, j: (0, j))
    return pl.pallas_call(
        functools.partial(_wprep_kernel, delta=delta),
        grid=(rows // tr, k // tk),
        in_specs=[pl.BlockSpec((tr, tk), lambda i, j: (rb + i, j)), pl.BlockSpec((LANE, tk), nxt)],
        out_specs=pl.BlockSpec((tr, tk), lambda i, j: (i, j)),
        out_shape=jax.ShapeDtypeStruct((rows, k), BF16),
        compiler_params=_params("parallel", "parallel"),
        name="wprep",
    )(wt, wt)


def _wsmall_kernel(a_ref, b_ref, o_ref, *, nh):
    row = lax.broadcasted_iota(jnp.int32, a_ref.shape, 0)
    o_ref[...] = jnp.where(row < nh, a_ref[...], jnp.where(row < 3 * nh, b_ref[...], 0.0)).astype(BF16)


def _wsmall(wt, row_f, row_ab, nh):
    assert row_f % LANE == 0 and row_ab % LANE == nh and 3 * nh <= LANE
    k = wt.shape[1]
    tk = _pick(k, 2048, LANE)
    return pl.pallas_call(
        functools.partial(_wsmall_kernel, nh=nh),
        grid=(k // tk,),
        in_specs=[pl.BlockSpec((LANE, tk), lambda j: (row_f // LANE, j)),
                  pl.BlockSpec((LANE, tk), lambda j: (row_ab // LANE, j))],
        out_specs=pl.BlockSpec((LANE, tk), lambda j: (0, j)),
        out_shape=jax.ShapeDtypeStruct((LANE, k), BF16),
        compiler_params=_params("parallel"),
        name="wsmall",
    )(wt, wt)


def _mm_kernel(x_ref, w_ref, o_ref):
    o_ref[...] = lax.dot_general(x_ref[...], w_ref[...], NT, preferred_element_type=F32).astype(o_ref.dtype)


def _matmul(x, wt, tm, tn, out_dtype=F32):
    m, k = x.shape
    n = wt.shape[0]
    return pl.pallas_call(
        _mm_kernel,
        grid=(m // tm, n // tn),
        in_specs=[pl.BlockSpec((tm, k), lambda i, j: (i, 0)),
                  pl.BlockSpec((tn, k), lambda i, j: (j, 0))],
        out_specs=pl.BlockSpec((tm, tn), lambda i, j: (i, j)),
        out_shape=jax.ShapeDtypeStruct((m, n), out_dtype),
        compiler_params=_params("parallel", "arbitrary"),
        name="inproj",
    )(x, wt)


def _small_kernel(x_ref, w_ref, p_ref, o_ref, *, nh):
    acc = lax.dot_general(x_ref[...], w_ref[...], NT, preferred_element_type=F32)
    xb = acc + p_ref[0:1, :]
    lane = lax.broadcasted_iota(jnp.int32, acc.shape, 1)
    sp = _softplus(xb)
    logf = jnp.minimum(xb, 0.0) - jnp.log1p(jnp.exp(-jnp.abs(xb)))
    g = -jnp.exp(p_ref[1:2, :]) * sp
    beta = jax.nn.sigmoid(acc)
    o_ref[...] = jnp.where(lane < nh, logf, jnp.where(lane < 2 * nh, g, jnp.where(lane < 3 * nh, beta, 0.0)))


def _small_heads(h, w_small, p_small, nh, tm):
    m, k = h.shape
    return pl.pallas_call(
        functools.partial(_small_kernel, nh=nh),
        grid=(m // tm,),
        in_specs=[pl.BlockSpec((tm, k), lambda i: (i, 0)),
                  pl.BlockSpec((LANE, k), lambda i: (0, 0)),
                  pl.BlockSpec((SUBLANE, LANE), lambda i: (0, 0))],
        out_specs=pl.BlockSpec((tm, LANE), lambda i: (i, 0)),
        out_shape=jax.ShapeDtypeStruct((m, LANE), F32),
        compiler_params=_params("parallel"),
        name="small_heads",
    )(h, w_small, p_small)


def _qknorm_kernel(x_ref, qg_ref, kg_ref, qn_ref, kn_ref, vb_ref, vt_ref, kf_ref, vf_ref, *, nq, nkv):
    def norm(x, g):
        return x * lax.rsqrt(jnp.mean(x * x, axis=-1, keepdims=True) + RMS_EPS) * g

    for h in range(nq):
        sl = slice(h * LANE, (h + 1) * LANE)
        qn_ref[:, sl] = norm(x_ref[:, sl], qg_ref[...]).astype(BF16)
    for h in range(nkv):
        sl = slice(h * LANE, (h + 1) * LANE)
        kx = norm(x_ref[:, (nq + h) * LANE:(nq + h + 1) * LANE], kg_ref[...])
        kf_ref[:, sl] = kx
        kn_ref[:, sl] = kx.astype(BF16)
        vx = x_ref[:, (nq + nkv + h) * LANE:(nq + nkv + h + 1) * LANE]
        vf_ref[:, sl] = vx
        vb_ref[:, sl] = vx.astype(BF16)
        vt_ref[sl, :] = vx.T.astype(BF16)


def _qknorm(proj, q_gain, k_gain, nq, nkv, tm):
    m = proj.shape[0]
    wq, wk = nq * LANE, nkv * LANE
    row = lambda w: pl.BlockSpec((tm, w), lambda i: (i, 0))
    gain = pl.BlockSpec((1, LANE), lambda i: (0, 0))
    return pl.pallas_call(
        functools.partial(_qknorm_kernel, nq=nq, nkv=nkv),
        grid=(m // tm,),
        in_specs=[row(wq + 2 * wk), gain, gain],
        out_specs=[row(wq), row(wk), row(wk), pl.BlockSpec((wk, tm), lambda i: (0, i)), row(wk), row(wk)],
        out_shape=[jax.ShapeDtypeStruct((m, wq), BF16), jax.ShapeDtypeStruct((m, wk), BF16),
                   jax.ShapeDtypeStruct((m, wk), BF16), jax.ShapeDtypeStruct((wk, m), BF16),
                   jax.ShapeDtypeStruct((m, wk), F32), jax.ShapeDtypeStruct((m, wk), F32)],
        compiler_params=_params("parallel"),
        name="qknorm",
    )(proj, q_gain.reshape(1, LANE), k_gain.reshape(1, LANE))


def _cumf_kernel(x_ref, o_ref, *, blk):
    t = x_ref.shape[0]
    r = lax.broadcasted_iota(jnp.int32, (blk, blk), 0)
    c = lax.broadcasted_iota(jnp.int32, (blk, blk), 1)
    tri = (r >= c).astype(BF16)
    carry = jnp.zeros((1, LANE), F32)
    for i in range(t // blk):
        cum = _dot_sel(tri, x_ref[i * blk:(i + 1) * blk, :]) + carry
        carry = cum[blk - 1:blk, :]
        o_ref[i * blk:(i + 1) * blk, :] = cum


def _cumf(small, nb, t):
    blk = _pick(t, 256, LANE)
    return pl.pallas_call(
        functools.partial(_cumf_kernel, blk=blk),
        grid=(nb,),
        in_specs=[pl.BlockSpec((t, LANE), lambda b: (b, 0))],
        out_specs=pl.BlockSpec((t, LANE), lambda b: (b, 0)),
        out_shape=jax.ShapeDtypeStruct((nb * t, LANE), F32),
        compiler_params=_params("parallel"),
        name="cumf",
    )(small)


def _fox_prompt_kernel(q_ref, k_ref, vt_ref, f_ref, o_ref, m_sc, l_sc, acc_sc, *, g, tq, scale):
    qi = pl.program_id(2)
    m_sc[...] = jnp.full(m_sc.shape, NEG_INF, F32)
    l_sc[...] = jnp.zeros(l_sc.shape, F32)
    acc_sc[...] = jnp.zeros(acc_sc.shape, F32)
    qs = jnp.concatenate([q_ref[:, i * LANE:(i + 1) * LANE] for i in range(g)], axis=0)

    def block(ki, masked):
        ks = pl.ds(pl.multiple_of(ki * tq, tq), tq)
        st = lax.dot_general(k_ref[ks, :], qs, NT, preferred_element_type=F32) * scale
        st = st - jnp.concatenate([jnp.broadcast_to(f_ref[0, 0, ks, i:i + 1], (tq, tq)) for i in range(g)], axis=1)
        if masked:
            keep = lax.broadcasted_iota(jnp.int32, (tq, tq), 0) <= lax.broadcasted_iota(jnp.int32, (tq, tq), 1)
            st = jnp.where(jnp.concatenate([keep] * g, axis=1), st, NEG_INF)
        m_prev = m_sc[...]
        m_new = jnp.maximum(m_prev, jnp.max(st, axis=0, keepdims=True))
        alpha = jnp.exp(m_prev - m_new)
        p = jnp.exp(st - m_new)
        l_sc[...] = alpha * l_sc[...] + jnp.sum(p, axis=0, keepdims=True)
        acc_sc[...] = alpha * acc_sc[...] + jnp.dot(vt_ref[:, ks], p.astype(BF16), preferred_element_type=F32)
        m_sc[...] = m_new

    def body(ki, carry):
        block(ki, False)
        return carry

    lax.fori_loop(0, qi, body, 0)
    block(qi, True)
    ot = acc_sc[...] / l_sc[...]
    for i in range(g):
        o_ref[:, i * LANE:(i + 1) * LANE] = ot[:, i * tq:(i + 1) * tq].T.astype(o_ref.dtype)


def _fox_prompt(qn, kn, vt, fcol, nb, t, nkv, g):
    tq = _pick(t, 512, LANE)
    nq = t // tq
    n = g * tq
    return pl.pallas_call(
        functools.partial(_fox_prompt_kernel, g=g, tq=tq, scale=LANE ** -0.5),
        grid=(nb, nkv, nq),
        in_specs=[pl.BlockSpec((tq, g * LANE), lambda b, h, qi: (b * nq + qi, h)),
                  pl.BlockSpec((t, LANE), lambda b, h, qi: (b, h)),
                  pl.BlockSpec((LANE, t), lambda b, h, qi: (h, b)),
                  pl.BlockSpec((1, 1, t, LANE), lambda b, h, qi: (b, h, 0, 0))],
        out_specs=pl.BlockSpec((tq, g * LANE), lambda b, h, qi: (b * nq + qi, h)),
        out_shape=jax.ShapeDtypeStruct((nb * t, nkv * g * LANE), BF16),
        scratch_shapes=[pltpu.VMEM((1, n), F32), pltpu.VMEM((1, n), F32), pltpu.VMEM((LANE, n), F32)],
        compiler_params=_params("parallel", "parallel", "arbitrary"),
        name="fox_prompt",
    )(qn, kn, vt, fcol)


def _fox_sample_kernel(pt_ref, q_ref, kn_ref, vn_ref, lfn_ref, *refs, npg, nkv, g, ts, scale):
    k_refs, v_refs, lf_refs = refs[:npg], refs[npg:2 * npg], refs[2 * npg:3 * npg]
    o_ref, m_sc, l_sc, acc_sc, f_sc = refs[3 * npg:]
    p = pl.program_id(1)
    page = k_refs[0].shape[1]
    nh = nkv * g
    rows = ts * nh
    cols = page * nkv

    @pl.when(p == 0)
    def _():
        m_sc[...] = jnp.full(m_sc.shape, NEG_INF, F32)
        l_sc[...] = jnp.zeros(l_sc.shape, F32)
        acc_sc[...] = jnp.zeros(acc_sc.shape, F32)
        f_sc[...] = jnp.zeros(f_sc.shape, F32)

    q = q_ref[0]

    def update(s, pv):
        m_prev = m_sc[...]
        m_new = jnp.maximum(m_prev, jnp.max(s, axis=-1, keepdims=True))
        alpha = jnp.exp(m_prev - m_new)
        pr = jnp.exp(s - m_new)
        l_sc[...] = alpha * l_sc[...] + jnp.sum(pr, axis=-1, keepdims=True)
        acc_sc[...] = alpha * acc_sc[...] + pv(pr.astype(BF16))
        m_sc[...] = m_new

    def own_kv(width):
        rq = lax.broadcasted_iota(jnp.int32, (rows, width), 0)
        cq = lax.broadcasted_iota(jnp.int32, (rows, width), 1)
        return rq, cq, (cq % nkv) == ((rq % nh) // g)

    ue = (lax.broadcasted_iota(jnp.int32, (page, cols), 0)
          <= lax.broadcasted_iota(jnp.int32, (page, cols), 1) // nkv).astype(BF16)
    _, _, own = own_kv(cols)
    carry = f_sc[...]
    parts = []
    for i in range(npg):
        fexp = _dot_sel(ue, lf_refs[i][0], sel_first=False) + carry[:, 0:1]
        carry = jnp.broadcast_to(fexp[:, cols - 1:cols], carry.shape)
        kf = k_refs[i][0].reshape(cols, LANE).astype(BF16)
        s = lax.dot_general(q, kf, NT, preferred_element_type=F32) * scale - jnp.concatenate([fexp] * ts, axis=0)
        parts.append(jnp.where(own, s, NEG_INF))
    f_sc[...] = carry

    def pv_pages(pr):
        out = None
        for i in range(npg):
            t_ = jnp.dot(pr[:, i * cols:(i + 1) * cols], v_refs[i][0].reshape(cols, LANE).astype(BF16),
                         preferred_element_type=F32)
            out = t_ if out is None else out + t_
        return out

    update(jnp.concatenate(parts, axis=1), pv_pages)

    @pl.when(p == pl.num_programs(1) - 1)
    def _():
        nn = kn_ref.shape[1]
        nl = lfn_ref.shape[1]
        uen = (lax.broadcasted_iota(jnp.int32, (nl, nn), 0)
               <= lax.broadcasted_iota(jnp.int32, (nl, nn), 1) // nkv).astype(BF16)
        fnew = _dot_sel(uen, lfn_ref[0], sel_first=False, dims=TN)[:nh] + f_sc[...][:, 0:1]
        rq, cq, own_n = own_kv(nn)
        keep = own_n & ((cq // nkv) <= (rq // nh)) & (cq < ts * nkv)
        s = lax.dot_general(q, kn_ref[0], NT, preferred_element_type=F32) * scale - jnp.concatenate([fnew] * ts, axis=0)
        update(jnp.where(keep, s, NEG_INF), lambda pr: jnp.dot(pr, vn_ref[0], preferred_element_type=F32))
        o_ref[0] = (acc_sc[...] / l_sc[...]).astype(o_ref.dtype)


def _fox_sample(page_ids, qb, knew, vnew, lfnew, kc, vc, lfc, nkv, g, ts):
    bs, rows, _ = qb.shape
    npages = page_ids.shape[1]
    npg = _pick(npages, 16, 1)
    page = kc.shape[1]
    nh = lfc.shape[1]
    nn = knew.shape[1]
    nl = lfnew.shape[1]

    def kv_page(i):
        return pl.BlockSpec((1, page, nkv, LANE), lambda b, p, pt, i=i: (pt[b * npages + p * npg + i], 0, 0, 0))

    def lf_page(i):
        return pl.BlockSpec((1, nh, page), lambda b, p, pt, i=i: (pt[b * npages + p * npg + i], 0, 0))

    per_b = lambda r: pl.BlockSpec((1, r, LANE), lambda b, p, pt: (b, 0, 0))
    in_specs = [per_b(rows), per_b(nn), per_b(nn), per_b(nl)]
    in_specs += [kv_page(i) for i in range(npg)] + [kv_page(i) for i in range(npg)] + [lf_page(i) for i in range(npg)]
    grid_spec = pltpu.PrefetchScalarGridSpec(
        num_scalar_prefetch=1,
        grid=(bs, npages // npg),
        in_specs=in_specs,
        out_specs=per_b(rows),
        scratch_shapes=[pltpu.VMEM((rows, 1), F32), pltpu.VMEM((rows, 1), F32), pltpu.VMEM((rows, LANE), F32),
                        pltpu.VMEM((nh, LANE), F32)],
    )
    return pl.pallas_call(
        functools.partial(_fox_sample_kernel, npg=npg, nkv=nkv, g=g, ts=ts, scale=LANE ** -0.5),
        grid_spec=grid_spec,
        out_shape=jax.ShapeDtypeStruct((bs, rows, LANE), BF16),
        compiler_params=_params("parallel", "arbitrary"),
        name="fox_sample",
    )(page_ids.reshape(-1), qb, knew, vnew, lfnew, *([kc] * npg), *([vc] * npg), *([lfc] * npg))


def _gdn_kernel(*refs, hb, c, taps, use_state, scale):
    if use_state:
        (q_ref, k_ref, v_ref, wq_ref, wk_ref, wv_ref, z_ref, gb_ref, nw_ref, hq_ref, hk_ref, hv_ref, s0_ref,
         o_ref, so_ref, s_sc, tail_sc) = refs
    else:
        q_ref, k_ref, v_ref, wq_ref, wk_ref, wv_ref, z_ref, gb_ref, nw_ref, o_ref, so_ref, s_sc, tail_sc = refs
    n = pl.program_id(2)

    @pl.when(n == 0)
    def _():
        s_sc[...] = s0_ref[0] if use_state else jnp.zeros(s_sc.shape, F32)
        if use_state:
            for i, h_ref in enumerate((hq_ref, hk_ref, hv_ref)):
                tail_sc[i] = h_ref[...]
        else:
            tail_sc[...] = jnp.zeros(tail_sc.shape, F32)

    heads = range(hb)

    def conv_act(x_ref, w_ref, idx, l2_scale):
        x = x_ref[...]
        ext = jnp.concatenate([tail_sc[idx], x], axis=0)
        tail_sc[idx] = x[c - SUBLANE:c]
        y = x * w_ref[taps - 1:taps, :]
        for back in range(1, taps):
            y = y + pltpu.roll(ext, back, 0)[SUBLANE:SUBLANE + c] * w_ref[taps - 1 - back:taps - back, :]
        y = _silu(y)
        cols = [y[:, h * LANE:(h + 1) * LANE] for h in heads]
        if l2_scale is not None:
            cols = [yh * (lax.rsqrt(jnp.sum(yh * yh, axis=-1, keepdims=True) + L2_EPS) * l2_scale) for yh in cols]
        return jnp.stack(cols)

    q = conv_act(q_ref, wq_ref, 0, scale)
    k = conv_act(k_ref, wk_ref, 1, 1.0)
    v = conv_act(v_ref, wv_ref, 2, None)
    gb = gb_ref[0]
    r = lax.broadcasted_iota(jnp.int32, (c, c), 0)
    cc = lax.broadcasted_iota(jnp.int32, (c, c), 1)
    incl, strict = r >= cc, r > cc
    gc_all = _dot_sel(incl.astype(BF16), gb)
    gc_t = gc_all.T
    gcol = jnp.stack([gc_all[:, h:h + 1] for h in heads])
    grow = jnp.stack([gc_t[h:h + 1, :] for h in heads])
    beta = jnp.stack([gb[:, hb + h:hb + h + 1] for h in heads])
    glast = gcol[:, c - 1:c, :]
    decay = jnp.exp(jnp.where(incl[None], gcol - grow, NEG_INF))
    kb = k * beta
    bdot = lambda a, b, eq: jnp.einsum(eq, a.astype(BF16), b.astype(BF16), preferred_element_type=F32)

    mm = jnp.where(strict[None], bdot(kb, k, 'hck,hsk->hcs') * decay, 0.0)
    eye = (r == cc).astype(F32)[None]
    base = min(c, SUBLANE)
    same = lambda s: ((r // s) == (cc // s))[None]
    mp = jnp.where(same(base), mm, 0.0)
    tinv = eye - mp
    span = 2
    while span < base:
        mp = bdot(mp, mp, 'hcs,hsk->hck')
        tinv = bdot(tinv, eye + mp, 'hcs,hsk->hck')
        span *= 2
    blk = base
    while blk < c:
        off = jnp.where(same(2 * blk) & ~same(blk), mm, 0.0)
        tinv = tinv - bdot(tinv, bdot(off, tinv, 'hcs,hsk->hck'), 'hcs,hsk->hck')
        blk *= 2
    egc = jnp.exp(gcol)
    u = bdot(tinv, v * beta, 'hcs,hsv->hcv')
    w = bdot(tinv, kb * egc, 'hcs,hsk->hck')
    a = bdot(q, k, 'hck,hsk->hcs') * decay
    qd = q * egc
    kd = k * jnp.exp(glast - gcol)
    s = s_sc[...]
    v_new = u - bdot(w, s, 'hck,hkv->hcv')
    o = bdot(qd, s, 'hck,hkv->hcv') + bdot(a, v_new, 'hcs,hsv->hcv')
    s_new = s * jnp.exp(glast) + bdot(kd, v_new, 'hck,hcv->hkv')
    s_sc[...] = s_new

    @pl.when(n == pl.num_programs(2) - 1)
    def _():
        so_ref[0] = s_new

    on = o * lax.rsqrt(jnp.mean(o * o, axis=-1, keepdims=True) + RMS_EPS) * nw_ref[...][None]
    for h in heads:
        sl = slice(h * LANE, (h + 1) * LANE)
        o_ref[:, sl] = (on[h] * _silu(z_ref[:, sl])).astype(o_ref.dtype)


def _gdn(x, conv_w, gb, norm_w, hist, s0, nb, tseq, c, hb, nheads):
    m = x.shape[0]
    n = tseq // c
    w = hb * LANE
    ng = nheads // hb
    kw = nheads * LANE
    taps = conv_w.shape[0]
    row_col = lambda cb: pl.BlockSpec((c, w), lambda b, hg, i, cb=cb: (b * n + i, cb + hg))
    wcol = lambda cb: pl.BlockSpec((taps, w), lambda b, hg, i, cb=cb: (0, cb + hg))
    in_specs = [row_col(0), row_col(kw // w), row_col(2 * kw // w),
                wcol(0), wcol(kw // w), wcol(2 * kw // w),
                row_col(3 * kw // w),
                pl.BlockSpec((1, c, LANE), lambda b, hg, i: (hg, b * n + i, 0)),
                pl.BlockSpec((1, LANE), lambda b, hg, i: (0, 0))]
    args = [x, x, x, conv_w, conv_w, conv_w, x, gb, norm_w.reshape(1, LANE)]
    state_spec = pl.BlockSpec((1, hb, LANE, LANE), lambda b, hg, i: (b, hg, 0, 0))
    if s0 is not None:
        hcol = lambda cb: pl.BlockSpec((SUBLANE, w), lambda b, hg, i, cb=cb: (b, cb + hg))
        in_specs += [hcol(0), hcol(kw // w), hcol(2 * kw // w), state_spec]
        args += [hist, hist, hist, s0]
    return pl.pallas_call(
        functools.partial(_gdn_kernel, hb=hb, c=c, taps=taps, use_state=s0 is not None, scale=LANE ** -0.5),
        grid=(nb, ng, n),
        in_specs=in_specs,
        out_specs=[pl.BlockSpec((c, w), lambda b, hg, i: (b * n + i, hg)), state_spec],
        out_shape=[jax.ShapeDtypeStruct((m, kw), BF16), jax.ShapeDtypeStruct((nb, nheads, LANE, LANE), F32)],
        scratch_shapes=[pltpu.VMEM((hb, LANE, LANE), F32), pltpu.VMEM((3, SUBLANE, w), F32)],
        compiler_params=_params("parallel", "parallel", "arbitrary"),
        name="gdn",
    )(*args)


def _merge_kernel(of_ref, og_ref, wf_ref, wg_ref, gf_ref, gg_ref, o_ref):
    a = jnp.dot(of_ref[...], wf_ref[...].astype(BF16), preferred_element_type=F32)
    b = jnp.dot(og_ref[...], wg_ref[...].astype(BF16), preferred_element_type=F32)
    o_ref[...] = (jax.nn.sigmoid(gf_ref[...]) * a + jax.nn.sigmoid(gg_ref[...]) * b).astype(o_ref.dtype)


def _merge(o_fox, o_gdn, w_bf, w_bg, proj, gf_col0, gg_col0, tm, tn):
    m, kf = o_fox.shape
    kg = o_gdn.shape[1]
    n = w_bf.shape[1]
    return pl.pallas_call(
        _merge_kernel,
        grid=(m // tm, n // tn),
        in_specs=[pl.BlockSpec((tm, kf), lambda i, j: (i, 0)),
                  pl.BlockSpec((tm, kg), lambda i, j: (i, 0)),
                  pl.BlockSpec((kf, tn), lambda i, j: (0, j)),
                  pl.BlockSpec((kg, tn), lambda i, j: (0, j)),
                  pl.BlockSpec((tm, tn), lambda i, j: (i, gf_col0 // tn + j)),
                  pl.BlockSpec((tm, tn), lambda i, j: (i, gg_col0 // tn + j))],
        out_specs=pl.BlockSpec((tm, tn), lambda i, j: (i, j)),
        out_shape=jax.ShapeDtypeStruct((m, n), BF16),
        compiler_params=_params("parallel", "arbitrary"),
        name="merge",
    )(o_fox, o_gdn, w_bf, w_bg, proj, proj)


def _resid_kernel(a_ref, w_ref, x_ref, gt_ref, o_ref):
    acc = jnp.dot(a_ref[...], w_ref[...].astype(BF16), preferred_element_type=F32)
    o_ref[0] = x_ref[0] + _rows(acc, gt_ref[0]) * acc


def _resid_matmul(a, w, x3, gt3, tiles_per_mod, tn, name):
    nt, tm, n = x3.shape
    k = a.shape[1]
    mr = gt3.shape[1]
    return pl.pallas_call(
        _resid_kernel,
        grid=(nt, n // tn),
        in_specs=[pl.BlockSpec((tm, k), lambda i, j: (i, 0)),
                  pl.BlockSpec((k, tn), lambda i, j: (0, j)),
                  pl.BlockSpec((1, tm, tn), lambda i, j: (i, 0, j)),
                  pl.BlockSpec((1, mr, tn), lambda i, j: (i // tiles_per_mod, 0, j))],
        out_specs=pl.BlockSpec((1, tm, tn), lambda i, j: (i, 0, j)),
        out_shape=jax.ShapeDtypeStruct((nt, tm, n), F32),
        compiler_params=_params("parallel", "arbitrary"),
        name=name,
    )(a, w, x3, gt3)


def _ffn_up_kernel(*refs, stride, taps, tiles_per_seq, halo_from_h):
    if halo_from_h:
        hh_ref, h_ref, wg_ref, wu_ref, cw_ref, act_ref, tail_ref, hcat = refs
    else:
        hist_ref, h_ref, wg_ref, wu_ref, cw_ref, act_ref, tail_ref = refs
    i, j = pl.program_id(0), pl.program_id(1)
    tm = h_ref.shape[0]
    if halo_from_h:
        hh = hh_ref.shape[0]

        @pl.when(j == 0)
        def _():
            hcat[0:hh, :] = jnp.where(i % tiles_per_seq == 0, jnp.zeros_like(hh_ref[...]), hh_ref[...])
            hcat[hh:hh + tm, :] = h_ref[...]

        ext = jnp.dot(hcat[...], wg_ref[...].astype(BF16), preferred_element_type=F32)
    else:
        hh = hist_ref.shape[0]
        ext = jnp.concatenate(
            [hist_ref[...], jnp.dot(h_ref[...], wg_ref[...].astype(BF16), preferred_element_type=F32)], axis=0)
    up = jnp.dot(h_ref[...], wu_ref[...].astype(BF16), preferred_element_type=F32)
    y = None
    for tap in range(taps):
        off = hh - (taps - 1 - tap) * stride
        term = ext[off:off + tm] * cw_ref[tap:tap + 1, :]
        y = term if y is None else y + term
    act_ref[...] = (_silu(y) * up).astype(act_ref.dtype)
    tail_ref[0] = ext[hh + tm - tail_ref.shape[1]:hh + tm]


def _ffn_up(h, hist, w_gate, w_up, conv_w, tm, tn, stride, tiles_per_seq, tail_rows):
    m, k = h.shape
    n = w_gate.shape[1]
    taps = conv_w.shape[0]
    halo_from_h = hist is None
    if halo_from_h:
        hh = BF16_ROWS
        first = pl.BlockSpec((hh, k), lambda i, j: (jnp.maximum(i * (tm // hh) - 1, 0), 0))
        first_arg = h
        scratch = [pltpu.VMEM((hh + tm, k), BF16)]
    else:
        hh = hist.shape[0]
        first = pl.BlockSpec((hh, tn), lambda i, j: (0, j))
        first_arg = hist
        scratch = []
    return pl.pallas_call(
        functools.partial(_ffn_up_kernel, stride=stride, taps=taps, tiles_per_seq=tiles_per_seq,
                          halo_from_h=halo_from_h),
        grid=(m // tm, n // tn),
        in_specs=[first,
                  pl.BlockSpec((tm, k), lambda i, j: (i, 0)),
                  pl.BlockSpec((k, tn), lambda i, j: (0, j)),
                  pl.BlockSpec((k, tn), lambda i, j: (0, j)),
                  pl.BlockSpec((taps, tn), lambda i, j: (0, j))],
        out_specs=[pl.BlockSpec((tm, tn), lambda i, j: (i, j)),
                   pl.BlockSpec((1, tail_rows, tn), lambda i, j: (i, 0, j))],
        out_shape=[jax.ShapeDtypeStruct((m, n), BF16), jax.ShapeDtypeStruct((m // tm, tail_rows, n), F32)],
        scratch_shapes=scratch,
        compiler_params=_params("parallel", "arbitrary"),
        name="ffn_up",
    )(first_arg, h, w_gate, w_up, conv_w)


def _layer(x2, mods, wts, lay, *, prompt, nb, tseq, fox_fn, gdn_s0, gdn_hist, ffn_hist):
    m, d = x2.shape
    sh_m, sc_m, gt_m, sh_f, sc_f, gt_f = mods
    nh, nkv, g, gh = lay["nh"], lay["nkv"], lay["g"], lay["gh"]
    kw = gh * LANE
    if prompt:
        tr = _pick(tseq, 256, SUBLANE)
        tm = _pick(tseq, 1024, LANE)
        stride = 1
    else:
        tr = nb
        tm = m
        stride = nb
    tiles_mod = (tseq // tr) if prompt else m // tr
    h = _normmod(x2.reshape(m // tr, tr, d), wts["norm_mix"], sc_m, sh_m, tiles_mod).reshape(m, d)
    mm = lambda wt: _matmul(h, wt, tm, _pick(wt.shape[0], 1024, LANE))
    proj_fox, proj_gdn, proj_gate = mm(wts["w_fox"]), mm(wts["w_gdn"]), mm(wts["w_gate"])
    small = _small_heads(h, wts["w_small"], wts["p_small"], nh, tm)
    qn, kn, vb, vt, k_f, v_f = _qknorm(proj_fox, wts["fox_q_norm"], wts["fox_k_norm"], nh, nkv, _pick(m, 256, LANE))
    o_fox = fox_fn(qn, kn, vb, vt, small)
    hb = lay["hb"]
    if prompt:
        o_gdn, s_fin = _gdn(proj_gdn, wts["gdn_conv_w"], _group_gb(small, nh, gh, hb), wts["gdn_norm"], None, None,
                            nb, tseq, lay["chunk"], hb, gh)
    else:
        ts = m // nb
        cpad = lay["cpad"]
        to_bm = lambda a: jnp.pad(a.reshape(ts, nb, -1).transpose(1, 0, 2),
                                  ((0, 0), (0, cpad - ts), (0, 0))).reshape(nb * cpad, -1)
        o_bm, s_fin = _gdn(to_bm(proj_gdn), wts["gdn_conv_w"], _group_gb(to_bm(small), nh, gh, hb), wts["gdn_norm"],
                           gdn_hist, gdn_s0, nb, cpad, cpad, hb, gh)
        o_gdn = o_bm.reshape(nb, cpad, kw)[:, :ts].transpose(1, 0, 2).reshape(m, kw)
    tn = _pick(d, 512, LANE)
    merged = _merge(o_fox, o_gdn, wts["w_branch_fox"], wts["w_branch_gdn"], proj_gate, 0, d, tm, tn)
    tiles_mod_m = (tseq // tm) if prompt else 1
    x3 = _resid_matmul(merged, wts["w_out"], x2.reshape(m // tm, tm, d), gt_m, tiles_mod_m, tn, "out_proj")
    x2 = x3.reshape(m, d)
    h2 = _normmod(x2.reshape(m // tr, tr, d), wts["norm_ffn"], sc_f, sh_f, tiles_mod).reshape(m, d)
    dff = wts["ffn_w_gate"].shape[1]
    tnf = _pick(dff, 256, LANE)
    act, tail = _ffn_up(h2, ffn_hist, wts["ffn_w_gate"], wts["ffn_w_up"], wts["ffn_conv_w"], tm, tnf, stride,
                        tseq // tm if prompt else 1, SUBLANE if prompt else m)
    tm2 = _pick(tm, 512, LANE) if prompt else m
    y3 = _resid_matmul(act, wts["ffn_w_down"], x2.reshape(m // tm2, tm2, d), gt_f,
                       (tseq // tm2) if prompt else 1, _pick(d, 512, LANE), "ffn_down")
    return y3.reshape(m, d), dict(gdn=proj_gdn, small=small, k=k_f, v=v_f, s=s_fin, tail=tail)


def _group_gb(small, nh, gh, hb):
    rows = small.shape[0]
    gg = small[:, nh:nh + gh].reshape(rows, gh // hb, hb)
    bb = small[:, nh + gh:nh + 2 * gh].reshape(rows, gh // hb, hb)
    gb = jnp.concatenate([gg, bb], axis=-1).transpose(1, 0, 2)
    return jnp.pad(gb, ((0, 0), (0, 0), (0, LANE - 2 * hb)))


def kernel(x_prompt, x_sample, cache_k, cache_v, cache_logf, state_gdn, state_gdn_conv, state_ffn_conv, page_table, c_prompt, c_sample, w_ada, b_ada, norm_mix, norm_ffn, w_in, fox_b_f, fox_q_norm, fox_k_norm, gdn_conv_w, gdn_A_log, gdn_dt_bias, gdn_norm, w_branch_fox, w_branch_gdn, w_out, ffn_w_gate, ffn_w_up, ffn_conv_w, ffn_w_down):
    nb, t, d = x_prompt.shape
    bs, ts, _ = x_sample.shape
    depth = w_in.shape[0]
    page, nkv = cache_k.shape[2], cache_k.shape[3]
    nh = cache_logf.shape[-1]
    g = nh // nkv
    gh = state_gdn.shape[2]
    kw = gh * LANE
    gconv = gdn_conv_w.shape[1]
    fconv = ffn_conv_w.shape[1]
    dff = ffn_w_gate.shape[-1]
    assert cache_k.shape[-1] == LANE and state_gdn.shape[-1] == LANE and state_gdn.shape[-2] == LANE
    assert gh == nh and ts >= gconv - 1 and ts >= fconv - 1

    sizes = (nh * LANE, nkv * LANE, nkv * LANE, nh, 3 * kw, kw, gh, gh, d, d)
    starts = [0]
    for s_ in sizes:
        starts.append(starts[-1] + s_)
    assert starts[-1] == w_in.shape[-1]
    hb = _pick(gh, 16, 1)
    lay = dict(nh=nh, nkv=nkv, g=g, gh=gh, hb=hb, chunk=_pick(t, 128, SUBLANE), cpad=SUBLANE)

    xs = x_sample.transpose(1, 0, 2).reshape(ts * bs, d)
    xp = x_prompt.reshape(nb * t, d)
    mc = -(-nb // SUBLANE) * SUBLANE
    c_all = jnp.concatenate([c_prompt, jnp.zeros((mc - nb, d), F32), c_sample], axis=0)

    outs_p = [[] for _ in range(6)]
    outs_s = [[] for _ in range(6)]
    for l in range(depth):
        nin = w_in.shape[2]
        wt = jnp.swapaxes(w_in, 1, 2).reshape(depth * nin, w_in.shape[1])
        run = lambda a, b: _wprep(wt, l * nin + starts[a], starts[b] - starts[a])
        w_small = _wsmall(wt, l * nin + starts[3], l * nin + starts[6], nh)
        zpad = jnp.zeros((LANE - 2 * nh,), F32)
        p_small = jnp.stack([jnp.concatenate([fox_b_f[l], gdn_dt_bias[l], zpad]),
                             jnp.concatenate([jnp.zeros((nh,), F32), gdn_A_log[l], zpad])])
        p_small = jnp.pad(p_small, ((0, SUBLANE - 2), (0, 0)))
        wts = dict(norm_mix=norm_mix[l], norm_ffn=norm_ffn[l], w_fox=run(0, 3), w_gdn=run(4, 6), w_gate=run(8, 10),
                   w_small=w_small, p_small=p_small,
                   fox_q_norm=fox_q_norm[l], fox_k_norm=fox_k_norm[l], gdn_conv_w=gdn_conv_w[l],
                   gdn_norm=gdn_norm[l], w_branch_fox=w_branch_fox[l], w_branch_gdn=w_branch_gdn[l],
                   w_out=w_out[l], ffn_w_gate=ffn_w_gate[l], ffn_w_up=ffn_w_up[l],
                   ffn_conv_w=ffn_conv_w[l], ffn_w_down=ffn_w_down[l].astype(BF16))

        mod = _ada(c_all, w_ada[l], b_ada[l])
        mods_p = [mod[:nb, i * d:(i + 1) * d].reshape(nb, 1, d) for i in range(6)]
        mods_s = [mod[mc:, i * d:(i + 1) * d].reshape(1, bs, d) for i in range(6)]

        def fox_p(qn, kn, vb, vt, small):
            f_tm = _cumf(small, nb, t)
            fcol = f_tm[:, :nh].reshape(nb, t, nkv, g).transpose(0, 2, 1, 3)
            fcol = jnp.pad(fcol, ((0, 0), (0, 0), (0, 0), (0, LANE - g)))
            return _fox_prompt(qn, kn, vt, fcol, nb, t, nkv, g)

        xp, st = _layer(xp, mods_p, wts, lay, prompt=True, nb=nb, tseq=t, fox_fn=fox_p, gdn_s0=None,
                        gdn_hist=None, ffn_hist=None)
        outs_p[0].append(st["k"].reshape(nb, t, nkv, LANE))
        outs_p[1].append(st["v"].reshape(nb, t, nkv, LANE))
        outs_p[2].append(st["small"].reshape(nb, t, LANE)[:, :, :nh])
        outs_p[3].append(st["s"])
        outs_p[4].append(st["gdn"].reshape(nb, t, -1)[:, t - (gconv - 1):, :3 * kw])
        tail = st["tail"].reshape(nb, -1, SUBLANE, dff)
        outs_p[5].append(tail[:, -1, SUBLANE - (fconv - 1):, :])

        n_pool = cache_k.shape[1]
        kc = cache_k.reshape(depth * n_pool, page, nkv, LANE)
        vc = cache_v.reshape(depth * n_pool, page, nkv, LANE)
        lfc = jnp.swapaxes(cache_logf, 2, 3).reshape(depth * n_pool, nh, page)
        page_ids = page_table + l * n_pool

        def fox_s(qn, kn, vb, vt, small):
            bm = lambda a, n: a.reshape(ts, bs, n, LANE).transpose(1, 0, 2, 3).reshape(bs, ts * n, LANE)
            nn = -(-ts * nkv // BF16_ROWS) * BF16_ROWS
            padr = lambda a, n: jnp.pad(a, ((0, 0), (0, n - a.shape[1]), (0, 0)))
            lfn = padr(small.reshape(ts, bs, LANE).transpose(1, 0, 2), BF16_ROWS)
            o = _fox_sample(page_ids, bm(qn, nh), padr(bm(kn, nkv), nn), padr(bm(vb, nkv), nn), lfn, kc, vc, lfc,
                            nkv, g, ts)
            return o.reshape(bs, ts, nh, LANE).transpose(1, 0, 2, 3).reshape(ts * bs, nh * LANE)

        ghist = jnp.pad(state_gdn_conv[l], ((0, 0), (SUBLANE - (gconv - 1), 0), (0, 0))).reshape(bs * SUBLANE, 3 * kw)
        fhist = state_ffn_conv[l].transpose(1, 0, 2).reshape((fconv - 1) * bs, dff)
        xs, st = _layer(xs, mods_s, wts, lay, prompt=False, nb=bs, tseq=ts, fox_fn=fox_s, gdn_s0=state_gdn[l],
                        gdn_hist=ghist, ffn_hist=fhist)
        bm3 = lambda a: a.reshape(ts, bs, -1).transpose(1, 0, 2)
        outs_s[0].append(bm3(st["k"]).reshape(bs, ts, nkv, LANE))
        outs_s[1].append(bm3(st["v"]).reshape(bs, ts, nkv, LANE))
        outs_s[2].append(bm3(st["small"])[:, :, :nh])
        outs_s[3].append(st["s"])
        graw = bm3(st["gdn"])[:, :, :3 * kw]
        outs_s[4].append(jnp.concatenate([state_gdn_conv[l], graw], axis=1)[:, -(gconv - 1):])
        fraw = bm3(st["tail"].reshape(ts * bs, dff))
        outs_s[5].append(jnp.concatenate([state_ffn_conv[l], fraw], axis=1)[:, -(fconv - 1):])

    y_prompt = xp.reshape(nb, t, d)
    y_sample = xs.reshape(ts, bs, d).transpose(1, 0, 2)
    return (y_prompt, y_sample, *(jnp.stack(a) for a in outs_p), *(jnp.stack(a) for a in outs_s))
```
